```python
import math
import jax, jax.numpy as jnp
from jax import lax
import numpy as np


D_MODEL = 1024
BATCH = 8
SEQ = 2048
DEPTH = 4

CHUNK = 64
D_MIX = D_MODEL
D_MLSTM = D_MIX // 2
D_S5 = D_MIX - D_MLSTM
MLSTM_HEADS = 4
MLSTM_HEAD_DIM = D_MLSTM // MLSTM_HEADS
CONV_WIDTH = 4
S5_GROUP = 16
S5_GROUPS = D_S5 // S5_GROUP
S5_STATE = 64
PLE_DIM = 256
N_EXPERT_GROUPS = 4
EXPERTS_PER_GROUP = 8
N_EXPERTS = N_EXPERT_GROUPS * EXPERTS_PER_GROUP
TOP_K = 2
D_EXPERT = D_MODEL // 2
MOE_BLOCK = 128
D_IN = 3 * D_MLSTM + 2 * MLSTM_HEADS + D_S5
DEEPNORM_ALPHA = (2 * DEPTH) ** 0.25
DEEPNORM_BETA = (8 * DEPTH) ** -0.25
EPS = 1e-5

kernel_name = 'hybrid_mlstm_s5_hmoe_deepnorm'


def layer_norm(x, g, b):
    xf = x.astype(jnp.float32)
    mu = xf.mean(-1, keepdims=True)
    var = jnp.square(xf - mu).mean(-1, keepdims=True)
    y = (xf - mu) * lax.rsqrt(var + EPS) * g.astype(jnp.float32) + b.astype(jnp.float32)
    return y.astype(x.dtype)


def rms_norm(x, g):
    xf = x.astype(jnp.float32)
    y = xf * lax.rsqrt(jnp.mean(xf * xf, -1, keepdims=True) + EPS) * g.astype(jnp.float32)
    return y.astype(x.dtype)


def head_norm(h, g):
    B, S, H, Dh = h.shape
    mu = h.mean(-1, keepdims=True)
    var = jnp.square(h - mu).mean(-1, keepdims=True)
    y = ((h - mu) * lax.rsqrt(var + EPS)).reshape(B, S, H * Dh)
    return y * g.astype(jnp.float32)


def causal_conv(u, w, b):
    C = u.shape[-1]
    y = lax.conv_general_dilated(u, w[:, None, :].astype(u.dtype), window_strides=(1,),
                                 padding=[(CONV_WIDTH - 1, 0)],
                                 dimension_numbers=('NWC', 'WIO', 'NWC'),
                                 feature_group_count=C)
    return y + b.astype(u.dtype)


def mlstm_chunkwise(q, k, v, i_pre, f_pre):
    B, S, H, Dh = q.shape
    NC = S // CHUNK
    f32 = jnp.float32

    def chunks(t):
        return t.astype(f32).reshape(B, NC, CHUNK, H, -1).transpose(0, 3, 1, 2, 4)

    qc, kc, vc = chunks(q), chunks(k), chunks(v)
    ig = chunks(i_pre[..., None])[..., 0]
    lf = jax.nn.log_sigmoid(chunks(f_pre[..., None])[..., 0])
    bcum = jnp.cumsum(lf, axis=-1)
    g = bcum[..., -1]
    a = g[..., None] - bcum + ig
    m_loc = a.max(-1)
    wk = jnp.exp(a - m_loc[..., None])[..., None] * kc
    chunk_C = jnp.einsum('bhcsk,bhcsv->bhckv', wk, vc)
    chunk_n = wk.sum(-2)

    def step(carry, inp):
        C, n, m = carry
        g_c, ml, Cc, nc = inp
        m_new = jnp.maximum(g_c + m, ml)
        s_old = jnp.exp(g_c + m - m_new)
        s_new = jnp.exp(ml - m_new)
        C_new = s_old[..., None, None] * C + s_new[..., None, None] * Cc
        n_new = s_old[..., None] * n + s_new[..., None] * nc
        return (C_new, n_new, m_new), (C, n, m)

    init = (jnp.zeros((B, H, Dh, Dh), f32), jnp.zeros((B, H, Dh), f32), jnp.zeros((B, H), f32))
    xs = (jnp.moveaxis(g, 2, 0), jnp.moveaxis(m_loc, 2, 0),
          jnp.moveaxis(chunk_C, 2, 0), jnp.moveaxis(chunk_n, 2, 0))
    _, (C_prev, n_prev, m_prev) = lax.scan(step, init, xs)
    C_prev = jnp.moveaxis(C_prev, 0, 2)
    n_prev = jnp.moveaxis(n_prev, 0, 2)
    m_prev = jnp.moveaxis(m_prev, 0, 2)

    causal = jnp.tril(jnp.ones((CHUNK, CHUNK), bool))
    log_d = jnp.where(causal, bcum[..., :, None] - bcum[..., None, :] + ig[..., None, :], -jnp.inf)
    log_inter = bcum + m_prev[..., None]
    m_t = jnp.maximum(log_inter, log_d.max(-1))
    s = jnp.einsum('bhctd,bhcsd->bhcts', qc, kc) * jnp.exp(log_d - m_t[..., None])
    inter = jnp.exp(log_inter - m_t)
    num = (jnp.einsum('bhcts,bhcsv->bhctv', s, vc)
           + inter[..., None] * jnp.einsum('bhctk,bhckv->bhctv', qc, C_prev))
    den = s.sum(-1) + inter * jnp.einsum('bhctk,bhck->bhct', qc, n_prev)
    den = jnp.maximum(jnp.abs(den), jnp.exp(-m_t))
    h = num / den[..., None]
    return h.transpose(0, 2, 3, 1, 4).reshape(B, S, H, Dh)


def complex_affine(e1, e2):
    a1r, a1i, b1r, b1i = e1
    a2r, a2i, b2r, b2i = e2
    return (a2r * a1r - a2i * a1i, a2r * a1i + a2i * a1r,
            a2r * b1r - a2i * b1i + b2r, a2r * b1i + a2i * b1r + b2i)


def s5_glu(u, lam_re, lam_im, log_dt, b_re, b_im, c_re, c_im, d_skip, w_glu, b_glu):
    B, S, _ = u.shape
    f32 = jnp.float32
    uf = u.astype(f32).reshape(B, S, S5_GROUPS, S5_GROUP)
    lr, li = lam_re.astype(f32), lam_im.astype(f32)
    dt = jnp.exp(log_dt.astype(f32))[:, None]
    er = jnp.exp(lr * dt)
    ar, ai = er * jnp.cos(li * dt), er * jnp.sin(li * dt)
    mag2 = lr * lr + li * li
    xr, xi = ar - 1.0, ai
    cr = (xr * lr + xi * li) / mag2
    ci = (xi * lr - xr * li) / mag2
    br, bi = b_re.astype(f32), b_im.astype(f32)
    bbr = cr[..., None] * br - ci[..., None] * bi
    bbi = cr[..., None] * bi + ci[..., None] * br
    bur = jnp.einsum('bsgc,gpc->bsgp', uf, bbr)
    bui = jnp.einsum('bsgc,gpc->bsgp', uf, bbi)
    a_r = jnp.broadcast_to(ar, bur.shape)
    a_i = jnp.broadcast_to(ai, bur.shape)
    _, _, sr, si = lax.associative_scan(complex_affine, (a_r, a_i, bur, bui), axis=1)
    y = (jnp.einsum('bsgp,gcp->bsgc', sr, c_re.astype(f32))
         - jnp.einsum('bsgp,gcp->bsgc', si, c_im.astype(f32))
         + d_skip.astype(f32) * uf).reshape(B, S, D_S5)
    gy = jax.nn.gelu(y)
    out = gy * jax.nn.sigmoid(gy @ w_glu.astype(f32) + b_glu.astype(f32))
    return out


def mixer(x, w_in, conv_w, conv_b, w_q, w_k, b_i, b_f, mh_g, lam_re, lam_im, log_dt,
          b_re, b_im, c_re, c_im, d_skip, w_glu, b_glu, s5_g, w_out):
    B, S, _ = x.shape
    H, Dh = MLSTM_HEADS, MLSTM_HEAD_DIM
    z = x @ w_in
    u_m, v, o_pre, i_pre, f_pre, u_s = jnp.split(
        z, [D_MLSTM, 2 * D_MLSTM, 3 * D_MLSTM, 3 * D_MLSTM + H, 3 * D_MLSTM + 2 * H], axis=-1)
    c = jax.nn.silu(causal_conv(u_m, conv_w, conv_b)).reshape(B, S, H, Dh)
    q = jnp.einsum('bshd,hde->bshe', c, w_q)
    k = jnp.einsum('bshd,hde->bshe', c, w_k) * (Dh ** -0.5)
    h = mlstm_chunkwise(q, k, v.reshape(B, S, H, Dh), i_pre + b_i, f_pre + b_f)
    h = jax.nn.sigmoid(o_pre.astype(jnp.float32)).reshape(B, S, H, Dh) * h
    y_m = head_norm(h, mh_g).astype(x.dtype)
    y_s = rms_norm(s5_glu(u_s, lam_re, lam_im, log_dt, b_re, b_im, c_re, c_im, d_skip,
                          w_glu, b_glu), s5_g).astype(x.dtype)
    return jnp.concatenate([y_m, y_s], axis=-1) @ w_out


def expert_dispatch(xf, e_idx, e_w, w_eg, w_eu, w_ed):
    M, D = xf.shape
    E = w_eg.shape[0]
    A = M * TOP_K
    flat_e = e_idx.reshape(A)
    flat_t = jnp.repeat(jnp.arange(M, dtype=jnp.int32), TOP_K)
    flat_w = e_w.reshape(A)
    order = jnp.argsort(flat_e)
    se, st, sw = flat_e[order], flat_t[order], flat_w[order]
    counts = jnp.bincount(flat_e, length=E)
    starts = jnp.cumsum(counts) - counts
    pcounts = (counts + MOE_BLOCK - 1) // MOE_BLOCK * MOE_BLOCK
    pends = jnp.cumsum(pcounts)
    pstarts = pends - pcounts
    dest = pstarts[se] + (jnp.arange(A) - starts[se])
    NB = -(-A // MOE_BLOCK) + E
    P = NB * MOE_BLOCK
    buf_t = jnp.full((P,), M, jnp.int32).at[dest].set(st)
    buf_w = jnp.zeros((P,), xf.dtype).at[dest].set(sw)
    blk_e = jnp.clip(jnp.searchsorted(pends, jnp.arange(NB) * MOE_BLOCK, side='right'), 0, E - 1)
    xpad = jnp.concatenate([xf, jnp.zeros((1, D), xf.dtype)], axis=0)

    def run_block(args):
        tok, w, e = args
        xb = xpad[tok]
        hb = jax.nn.silu(xb @ w_eg[e]) * (xb @ w_eu[e])
        return (hb @ w_ed[e]) * w[:, None]

    out = lax.map(run_block, (buf_t.reshape(NB, MOE_BLOCK), buf_w.reshape(NB, MOE_BLOCK), blk_e))
    y = jnp.zeros((M + 1, D), xf.dtype).at[buf_t].add(out.reshape(P, D))
    return y[:M]


def moe_ffn(x, w_grp, b_grp, w_rt, b_rt, w_eg, w_eu, w_ed):
    B, S, D = x.shape
    M = B * S
    xf = x.reshape(M, D)
    grp_prob = jax.nn.softmax((xf @ w_grp + b_grp).astype(jnp.float32), axis=-1)
    grp = jnp.argmax(grp_prob, axis=-1).astype(jnp.int32)
    p_grp = jnp.take_along_axis(grp_prob, grp[:, None], axis=-1)
    e_logits = (xf @ w_rt + b_rt).astype(jnp.float32).reshape(M, N_EXPERT_GROUPS, EXPERTS_PER_GROUP)
    e_in = jnp.take_along_axis(e_logits, grp[:, None, None], axis=1)[:, 0]
    top_v, top_i = lax.top_k(e_in, TOP_K)
    e_w = (p_grp * jax.nn.softmax(top_v, axis=-1)).astype(x.dtype)
    e_idx = grp[:, None] * EXPERTS_PER_GROUP + top_i.astype(jnp.int32)
    return expert_dispatch(xf, e_idx, e_w, w_eg, w_eu, w_ed).reshape(B, S, D)


def setup_inputs(seed: int = 0) -> dict:
    key = jax.random.key(seed)
    ks = iter(jax.random.split(key, 48))
    L, H, Dh, G, P = DEPTH, MLSTM_HEADS, MLSTM_HEAD_DIM, S5_GROUPS, S5_STATE

    def nrm(shape, scale):
        return scale * jax.random.normal(next(ks), shape, jnp.float32)

    d = {}
    d['x'] = nrm((BATCH, SEQ, D_MODEL), 1.0)
    d['p'] = nrm((DEPTH, BATCH, SEQ, PLE_DIM), 1.0)
    d['w_in'] = nrm((L, D_MODEL, D_IN), D_MODEL ** -0.5)
    d['conv_w'] = nrm((L, CONV_WIDTH, D_MLSTM), CONV_WIDTH ** -0.5)
    d['conv_b'] = nrm((L, D_MLSTM), 0.02)
    d['w_q'] = nrm((L, H, Dh, Dh), Dh ** -0.5)
    d['w_k'] = nrm((L, H, Dh, Dh), Dh ** -0.5)
    d['b_i'] = nrm((L, H), 0.1)
    d['b_f'] = jnp.linspace(3.0, 6.0, H, dtype=jnp.float32)[None, :] + nrm((L, H), 0.1)
    d['mh_g'] = 1.0 + nrm((L, D_MLSTM), 0.02)
    d['lam_re'] = -0.5 + nrm((L, G, P), 0.01)
    d['lam_im'] = jnp.pi * jnp.arange(P, dtype=jnp.float32)[None, None, :] + nrm((L, G, P), 0.01)
    d['log_dt'] = jax.random.uniform(next(ks), (L, G), jnp.float32, math.log(1e-3), math.log(1e-1))
    d['b_re'] = nrm((L, G, P, S5_GROUP), (2 * S5_GROUP) ** -0.5)
    d['b_im'] = nrm((L, G, P, S5_GROUP), (2 * S5_GROUP) ** -0.5)
    d['c_re'] = nrm((L, G, S5_GROUP, P), (2 * P) ** -0.5)
    d['c_im'] = nrm((L, G, S5_GROUP, P), (2 * P) ** -0.5)
    d['d_skip'] = nrm((L, G, S5_GROUP), 1.0)
    d['w_glu'] = nrm((L, D_S5, D_S5), D_S5 ** -0.5)
    d['b_glu'] = nrm((L, D_S5), 0.02)
    d['s5_g'] = 1.0 + nrm((L, D_S5), 0.02)
    d['w_out'] = nrm((L, D_MIX, D_MODEL), DEEPNORM_BETA * D_MIX ** -0.5)
    d['ln1_g'] = 1.0 + nrm((L, D_MODEL), 0.02)
    d['ln1_b'] = nrm((L, D_MODEL), 0.02)
    d['w_grp'] = nrm((L, D_MODEL, N_EXPERT_GROUPS), D_MODEL ** -0.5)
    d['b_grp'] = nrm((L, N_EXPERT_GROUPS), 0.01)
    d['w_rt'] = nrm((L, D_MODEL, N_EXPERTS), D_MODEL ** -0.5)
    d['b_rt'] = nrm((L, N_EXPERTS), 0.01)
    d['w_eg'] = nrm((L, N_EXPERTS, D_MODEL, D_EXPERT), D_MODEL ** -0.5)
    d['w_eu'] = nrm((L, N_EXPERTS, D_MODEL, D_EXPERT), D_MODEL ** -0.5)
    d['w_ed'] = nrm((L, N_EXPERTS, D_EXPERT, D_MODEL), DEEPNORM_BETA * D_EXPERT ** -0.5)
    d['ln2_g'] = 1.0 + nrm((L, D_MODEL), 0.02)
    d['ln2_b'] = nrm((L, D_MODEL), 0.02)
    d['w_pg'] = nrm((L, D_MODEL, D_MODEL), D_MODEL ** -0.5)
    d['b_pg'] = nrm((L, D_MODEL), 0.02)
    d['w_pp'] = nrm((L, PLE_DIM, D_MODEL), PLE_DIM ** -0.5)
    d['ple_g'] = 1.0 + nrm((L, D_MODEL), 0.02)
    return d


def reference(x, p, w_in, conv_w, conv_b, w_q, w_k, b_i, b_f, mh_g, lam_re, lam_im, log_dt,
              b_re, b_im, c_re, c_im, d_skip, w_glu, b_glu, s5_g, w_out, ln1_g, ln1_b,
              w_grp, b_grp, w_rt, b_rt, w_eg, w_eu, w_ed, ln2_g, ln2_b, w_pg, b_pg, w_pp, ple_g):
    for i in range(DEPTH):
        mix = mixer(x, w_in[i], conv_w[i], conv_b[i], w_q[i], w_k[i], b_i[i], b_f[i], mh_g[i],
                    lam_re[i], lam_im[i], log_dt[i], b_re[i], b_im[i], c_re[i], c_im[i],
                    d_skip[i], w_glu[i], b_glu[i], s5_g[i], w_out[i])
        x = layer_norm(DEEPNORM_ALPHA * x + mix, ln1_g[i], ln1_b[i])
        ffn = moe_ffn(x, w_grp[i], b_grp[i], w_rt[i], b_rt[i], w_eg[i], w_eu[i], w_ed[i])
        x = layer_norm(DEEPNORM_ALPHA * x + ffn, ln2_g[i], ln2_b[i])
        gate = jax.nn.sigmoid(x @ w_pg[i] + b_pg[i])
        ple = rms_norm(p[i] @ w_pp[i], ple_g[i])
        x = x + gate * ple
    return x
```

```python
import functools
import math

import jax
import jax.numpy as jnp
from jax import lax
from jax.experimental import pallas as pl
from jax.experimental.pallas import tpu as pltpu

F32 = jnp.float32
BF16 = jnp.bfloat16
I32 = jnp.int32
HIGHEST = lax.Precision.HIGHEST

EPS = 1e-5
V7X_LANES = 128
V7X_SUBLANES = 8
VMEM_LIMIT = 56 * 1024 * 1024

MLSTM_CHUNK = 64
S5_CHUNK = 16
MOE_ROWS = 256
TOP_K = 2


def _cparams(sem):
    return pltpu.CompilerParams(dimension_semantics=sem, vmem_limit_bytes=VMEM_LIMIT)


def _iota(shape, axis):
    return lax.broadcasted_iota(I32, shape, axis)


def _inproj_kernel(x_ref, wm_ref, wg_ref, zm_ref, zg_ref):
    xb = x_ref[...].astype(BF16)
    zm_ref[...] = jnp.dot(xb, wm_ref[...], preferred_element_type=F32)
    zg_ref[...] = jnp.dot(xb, wg_ref[...], preferred_element_type=F32)


def _inproj(x2d, w_main, w_gate, tm=512):
    M, D = x2d.shape
    N = w_main.shape[1]
    return pl.pallas_call(
        _inproj_kernel,
        grid=(M // tm,),
        in_specs=[pl.BlockSpec((tm, D), lambda i: (i, 0)),
                  pl.BlockSpec((D, N), lambda i: (0, 0)),
                  pl.BlockSpec((D, V7X_LANES), lambda i: (0, 0))],
        out_specs=[pl.BlockSpec((tm, N), lambda i: (i, 0)),
                   pl.BlockSpec((tm, V7X_LANES), lambda i: (i, 0))],
        out_shape=[jax.ShapeDtypeStruct((M, N), F32),
                   jax.ShapeDtypeStruct((M, V7X_LANES), F32)],
        compiler_params=_cparams(("parallel",)),
        name="inproj",
    )(x2d, w_main, w_gate)


def _mlstm_kernel(bif_ref, um_ref, v_ref, o_ref, gt_ref, cw_ref, cb_ref, wq_ref, wk_ref, mhg_ref,
                  y_ref, q_s, k_s, bcum_s, an_s, r_s, g_s, ml_s):
    h = pl.program_id(1)
    S, Dh = um_ref.shape
    NC, L = gt_ref.shape[1], gt_ref.shape[2]

    u = um_ref[...]
    row = _iota((S, Dh), 0)
    width = cw_ref.shape[0]
    acc = cb_ref[...] + cw_ref[width - 1:width, :] * u
    for j in range(width - 1):
        sh = width - 1 - j
        us = jnp.where(row >= sh, pltpu.roll(u, sh, axis=0), 0.0)
        acc = acc + cw_ref[j:j + 1, :] * us
    cb = (acc * jax.nn.sigmoid(acc)).astype(BF16)
    q_s[...] = jnp.dot(cb, wq_ref[...], preferred_element_type=F32)
    k_s[...] = jnp.dot(cb, wk_ref[...], preferred_element_type=F32) * (Dh ** -0.5)

    ig = gt_ref[0] + bif_ref[0, h]
    fp = gt_ref[1] + bif_ref[1, h]
    lf = jnp.minimum(fp, 0.0) - jnp.log1p(jnp.exp(-jnp.abs(fp)))
    tri = jnp.where(_iota((L, L), 0) <= _iota((L, L), 1), 1.0, 0.0).astype(F32)
    bcum = jnp.dot(lf, tri, precision=HIGHEST, preferred_element_type=F32)
    g = bcum[:, L - 1:L]
    a = g - bcum + ig
    m_loc = jnp.max(a, axis=-1, keepdims=True)
    bcum_s[...] = bcum
    an_s[...] = a - m_loc
    r_s[...] = ig - bcum
    g_s[...] = jnp.broadcast_to(g, (NC, V7X_LANES))
    ml_s[...] = jnp.broadcast_to(m_loc, (NC, V7X_LANES))

    eye = _iota((L, L), 0) == _iota((L, L), 1)
    causal = _iota((L, L), 0) >= _iota((L, L), 1)

    def to_col(row_vec):
        return jnp.sum(jnp.where(eye, row_vec, 0.0), axis=1, keepdims=True)

    def chunk(c, carry):
        C, n, m = carry
        t0 = pl.multiple_of(c * L, L)
        b_col = to_col(bcum_s[pl.ds(c, 1), :])
        an_col = to_col(an_s[pl.ds(c, 1), :])
        r_row = r_s[pl.ds(c, 1), :]
        g_c = g_s[pl.ds(c, 1), :][:, 0:1]
        ml_c = ml_s[pl.ds(c, 1), :][:, 0:1]
        qc = q_s[pl.ds(t0, L), :]
        kc = k_s[pl.ds(t0, L), :]
        vb = v_ref[pl.ds(t0, L), :].astype(BF16)
        qb = qc.astype(BF16)

        log_d = jnp.where(causal, b_col + r_row, -jnp.inf)
        log_inter = b_col + m
        m_t = jnp.maximum(log_inter, jnp.max(log_d, axis=1, keepdims=True))
        s = lax.dot_general(qb, kc.astype(BF16), (((1,), (1,)), ((), ())),
                            preferred_element_type=F32) * jnp.exp(log_d - m_t)
        inter = jnp.exp(log_inter - m_t)
        num = (jnp.dot(s.astype(BF16), vb, preferred_element_type=F32)
               + inter * jnp.dot(qb, C.astype(BF16), preferred_element_type=F32))
        den = jnp.sum(s, axis=1, keepdims=True) + inter * jnp.sum(qc * n, axis=1, keepdims=True)
        den = jnp.maximum(jnp.abs(den), jnp.exp(-m_t))
        hh = jax.nn.sigmoid(o_ref[pl.ds(t0, L), :]) * (num / den)
        mu = jnp.mean(hh, axis=-1, keepdims=True)
        hc = hh - mu
        var = jnp.mean(hc * hc, axis=-1, keepdims=True)
        y_ref[pl.ds(t0, L), :] = hc * lax.rsqrt(var + EPS) * mhg_ref[...]

        wk = jnp.exp(an_col) * kc
        Cc = lax.dot_general(wk.astype(BF16), vb, (((0,), (0,)), ((), ())), preferred_element_type=F32)
        nc = jnp.sum(wk, axis=0, keepdims=True)
        m_new = jnp.maximum(g_c + m, ml_c)
        s_old = jnp.exp(g_c + m - m_new)
        s_new = jnp.exp(ml_c - m_new)
        return s_old * C + s_new * Cc, s_old * n + s_new * nc, m_new

    init = (jnp.zeros((Dh, Dh), F32), jnp.zeros((1, Dh), F32), jnp.zeros((1, 1), F32))
    lax.fori_loop(0, NC, chunk, init)


def _mlstm(z3, gates, b_if, conv_w, conv_b, wq, wk, mh_g, H, Dh):
    B, S, _ = z3.shape
    NC, L = gates.shape[3], gates.shape[4]
    blk = lambda off: pl.BlockSpec((None, S, Dh), lambda b, h: (b, 0, off + h))
    return pl.pallas_call(
        _mlstm_kernel,
        grid=(B, H),
        in_specs=[pl.BlockSpec(memory_space=pltpu.SMEM),
                  blk(0), blk(H), blk(2 * H),
                  pl.BlockSpec((None, None, 2, NC, L), lambda b, h: (b, h, 0, 0, 0)),
                  pl.BlockSpec((conv_w.shape[0], Dh), lambda b, h: (0, h)),
                  pl.BlockSpec((1, Dh), lambda b, h: (0, h)),
                  pl.BlockSpec((None, Dh, Dh), lambda b, h: (h, 0, 0)),
                  pl.BlockSpec((None, Dh, Dh), lambda b, h: (h, 0, 0)),
                  pl.BlockSpec((1, Dh), lambda b, h: (0, h))],
        out_specs=pl.BlockSpec((None, S, Dh), lambda b, h: (b, 0, h)),
        out_shape=jax.ShapeDtypeStruct((B, S, H * Dh), F32),
        scratch_shapes=[pltpu.VMEM((S, Dh), F32), pltpu.VMEM((S, Dh), F32),
                        pltpu.VMEM((NC, L), F32), pltpu.VMEM((NC, L), F32), pltpu.VMEM((NC, L), F32),
                        pltpu.VMEM((NC, V7X_LANES), F32), pltpu.VMEM((NC, V7X_LANES), F32)],
        compiler_params=_cparams(("parallel", "parallel")),
        name="mlstm",
    )(b_if, z3, z3, z3, gates, conv_w, conv_b, wq, wk, mh_g)


def _s5_operators(lam_re, lam_im, log_dt, b_re, b_im, c_re, c_im, d_skip, Lc):
    G, P = lam_re.shape
    Cn = b_re.shape[-1]
    lr, li = lam_re.astype(F32), lam_im.astype(F32)
    dt = jnp.exp(log_dt.astype(F32))[:, None]
    er = jnp.exp(lr * dt)
    ar, ai = er * jnp.cos(li * dt), er * jnp.sin(li * dt)
    mag2 = lr * lr + li * li
    xr, xi = ar - 1.0, ai
    cr = (xr * lr + xi * li) / mag2
    ci = (xi * lr - xr * li) / mag2
    bbr = cr[..., None] * b_re - ci[..., None] * b_im
    bbi = cr[..., None] * b_im + ci[..., None] * b_re
    pr, pi = [jnp.ones_like(ar)], [jnp.zeros_like(ai)]
    for _ in range(Lc):
        pr.append(pr[-1] * ar - pi[-1] * ai)
        pi.append(pr[-2] * ai + pi[-1] * ar)
    pwr, pwi = jnp.stack(pr, -1), jnp.stack(pi, -1)

    rev_r = pwr[:, :, Lc - 1::-1] if Lc > 1 else pwr[:, :, :1]
    rev_i = pwi[:, :, Lc - 1::-1] if Lc > 1 else pwi[:, :, :1]
    wr = jnp.einsum('gpt,gpc->gtcp', rev_r, bbr) - jnp.einsum('gpt,gpc->gtcp', rev_i, bbi)
    wi = jnp.einsum('gpt,gpc->gtcp', rev_r, bbi) + jnp.einsum('gpt,gpc->gtcp', rev_i, bbr)
    w2 = jnp.concatenate([wr, wi, wi, wr], axis=-1).reshape(G, Lc * Cn, 4 * P)

    car = c_re[..., None] * pwr[:, None] - c_im[..., None] * pwi[:, None]
    cai = c_re[..., None] * pwi[:, None] + c_im[..., None] * pwr[:, None]
    v_re = car[..., 1:].transpose(0, 2, 3, 1)
    v_im = -cai[..., 1:].transpose(0, 2, 3, 1)
    vmat = jnp.concatenate([v_re, v_im], axis=1).reshape(G, 2 * P, Lc * Cn)

    kern = (jnp.einsum('gcpk,gpd->gkdc', car[..., :Lc], bbr, precision=HIGHEST)
            - jnp.einsum('gcpk,gpd->gkdc', cai[..., :Lc], bbi, precision=HIGHEST))
    lag = jnp.arange(Lc)[None, :] - jnp.arange(Lc)[:, None]
    tm = jnp.where((lag >= 0)[None, :, :, None, None], kern[:, jnp.clip(lag, 0, Lc - 1)], 0.0)
    tmat = tm.transpose(0, 1, 3, 2, 4).reshape(G, Lc * Cn, Lc * Cn)

    pL, qL = pwr[..., Lc], pwi[..., Lc]
    pq = jnp.stack([jnp.concatenate([pL, pL], -1), jnp.concatenate([-qL, qL], -1),
                    jnp.concatenate([qL, -qL], -1)], axis=1)
    pq = jnp.concatenate([pq, jnp.zeros((G, V7X_SUBLANES - 3, 2 * P), F32)], axis=1)
    dtile = jnp.tile(d_skip.astype(F32), (1, Lc))[:, None, :]
    return w2.astype(BF16), tmat.astype(BF16), vmat.astype(BF16), pq, dtile


def _s5_kernel(u_ref, w2_ref, t_ref, v_ref, pq_ref, d_ref, y_ref, z_s, xp_s, *, nb):
    R = u_ref.shape[0]
    P2 = xp_s.shape[1]
    u = u_ref[...]
    ub = u.astype(BF16)
    z_s[...] = jnp.dot(ub, w2_ref[...], preferred_element_type=F32)
    pv, qv, q2 = pq_ref[0:1, :], pq_ref[1:2, :], pq_ref[2:3, :]

    def step(c, carry):
        x, xs = carry
        r0 = pl.multiple_of(c * nb, nb)
        xp_s[pl.ds(r0, nb), :] = x
        z = z_s[pl.ds(r0, nb), :]
        return x * pv + xs * qv + z[:, :P2], xs * pv + x * q2 + z[:, P2:]

    zero = jnp.zeros((nb, P2), F32)
    lax.fori_loop(0, R // nb, step, (zero, zero), unroll=8)
    y_ref[...] = (jnp.dot(ub, t_ref[...], preferred_element_type=F32)
                  + jnp.dot(xp_s[...].astype(BF16), v_ref[...], preferred_element_type=F32)
                  + u * d_ref[...])


def _s5(ug, w2, tmat, vmat, pq, dtile, nb):
    G, R, W = ug.shape
    P2 = vmat.shape[1]
    sq = lambda *shape: pl.BlockSpec((None,) + shape, lambda g: (g,) + (0,) * len(shape))
    return pl.pallas_call(
        functools.partial(_s5_kernel, nb=nb),
        grid=(G,),
        in_specs=[sq(R, W), sq(W, 2 * P2), sq(W, W), sq(P2, W), sq(V7X_SUBLANES, P2), sq(1, W)],
        out_specs=sq(R, W),
        out_shape=jax.ShapeDtypeStruct((G, R, W), F32),
        scratch_shapes=[pltpu.VMEM((R, 2 * P2), F32), pltpu.VMEM((R, P2), F32)],
        compiler_params=_cparams(("parallel",)),
        name="s5",
    )(ug, w2, tmat, vmat, pq, dtile)


def _layer_norm(v, g, b):
    mu = jnp.mean(v, axis=-1, keepdims=True)
    vc = v - mu
    var = jnp.mean(vc * vc, axis=-1, keepdims=True)
    return vc * lax.rsqrt(var + EPS) * g + b


def _mixout_kernel(ym_ref, ys_ref, x_ref, wglu_ref, bglu_ref, s5g_ref, wom_ref, wos_ref,
                   g1_ref, b1_ref, wr_ref, br_ref, x1_ref, ri_ref, rw_ref, *, alpha, n_grp, epg):
    ys = ys_ref[...]
    gy = 0.5 * ys * (1.0 + jnp.tanh(math.sqrt(2.0 / math.pi) * (ys + 0.044715 * (ys * ys * ys))))
    glu = gy * jax.nn.sigmoid(jnp.dot(gy.astype(BF16), wglu_ref[...], preferred_element_type=F32)
                              + bglu_ref[...])
    ysn = glu * lax.rsqrt(jnp.mean(glu * glu, axis=-1, keepdims=True) + EPS) * s5g_ref[...]
    mix = (jnp.dot(ym_ref[...].astype(BF16), wom_ref[...], preferred_element_type=F32)
           + jnp.dot(ysn.astype(BF16), wos_ref[...], preferred_element_type=F32))
    x1 = _layer_norm(alpha * x_ref[...] + mix, g1_ref[...], b1_ref[...])
    x1_ref[...] = x1

    logits = jnp.dot(x1, wr_ref[...], precision=HIGHEST, preferred_element_type=F32) + br_ref[...]
    tm = logits.shape[0]
    lane = _iota((tm, V7X_LANES), 1)
    big = jnp.int32(V7X_LANES)
    neg = -jnp.inf
    gl = jnp.where(lane < n_grp, logits, neg)
    gmax = jnp.max(gl, axis=-1, keepdims=True)
    grp = jnp.min(jnp.where(gl == gmax, lane, big), axis=-1, keepdims=True)
    p_grp = 1.0 / jnp.sum(jnp.exp(gl - gmax), axis=-1, keepdims=True)
    lo = n_grp + grp * epg
    el = jnp.where((lane >= lo) & (lane < lo + epg), logits, neg)
    v1 = jnp.max(el, axis=-1, keepdims=True)
    i1 = jnp.min(jnp.where(el == v1, lane, big), axis=-1, keepdims=True)
    el2 = jnp.where(lane == i1, neg, el)
    v2 = jnp.max(el2, axis=-1, keepdims=True)
    i2 = jnp.min(jnp.where(el2 == v2, lane, big), axis=-1, keepdims=True)
    e2 = jnp.exp(v2 - v1)
    w1 = p_grp / (1.0 + e2)
    w2 = p_grp * e2 / (1.0 + e2)
    ri_ref[...] = jnp.where(lane == 0, i1 - n_grp, jnp.where(lane == 1, i2 - n_grp, 0))
    rw_ref[...] = jnp.where(lane == 0, w1, jnp.where(lane == 1, w2, 0.0))


def _mixout(ym, ys, x2d, wglu, bglu, s5g, wom, wos, g1, b1, wr, br, alpha, n_grp, epg, tm=512):
    M, D = x2d.shape
    Dm, Ds = ym.shape[1], ys.shape[1]
    tile = lambda w: pl.BlockSpec((tm, w), lambda i: (i, 0))
    full = lambda a: pl.BlockSpec(a.shape, lambda i: (0,) * a.ndim)
    return pl.pallas_call(
        functools.partial(_mixout_kernel, alpha=alpha, n_grp=n_grp, epg=epg),
        grid=(M // tm,),
        in_specs=[tile(Dm), tile(Ds), tile(D), full(wglu), full(bglu), full(s5g), full(wom), full(wos),
                  full(g1), full(b1), full(wr), full(br)],
        out_specs=[tile(D), tile(V7X_LANES), tile(V7X_LANES)],
        out_shape=[jax.ShapeDtypeStruct((M, D), F32),
                   jax.ShapeDtypeStruct((M, V7X_LANES), I32),
                   jax.ShapeDtypeStruct((M, V7X_LANES), F32)],
        compiler_params=_cparams(("parallel",)),
        name="mixout",
    )(ym, ys, x2d, wglu, bglu, s5g, wom, wos, g1, b1, wr, br)


def _dispatch_kernel(ri_ref, dest_ref, blk_ref, run_s, pst_s, *, n_exp, rows, nbp):
    ph = pl.program_id(0)
    i = pl.program_id(1)
    tm = ri_ref.shape[0]
    lane = _iota((tm, V7X_LANES), 1)
    oh0 = lane == ri_ref[:, 0:1]
    oh1 = lane == ri_ref[:, 1:2]
    cnt = jnp.where(oh0 | oh1, 1.0, 0.0).astype(F32)
    tile_cnt = jnp.sum(cnt, axis=0, keepdims=True)

    @pl.when((ph == 0) & (i == 0))
    def _():
        run_s[...] = jnp.zeros_like(run_s)

    @pl.when(ph == 0)
    def _():
        run_s[...] = run_s[...] + tile_cnt
        dest_ref[...] = jnp.zeros_like(dest_ref)

    @pl.when((ph == 1) & (i == 0))
    def _():
        counts = run_s[...]
        nblk = jnp.floor((counts + (rows - 1)) / rows)
        tri = jnp.where(_iota((V7X_LANES, V7X_LANES), 0) < _iota((V7X_LANES, V7X_LANES), 1), 1.0, 0.0)
        bstart = jnp.dot(jnp.broadcast_to(nblk, (V7X_SUBLANES, V7X_LANES)), tri.astype(F32),
                         precision=HIGHEST, preferred_element_type=F32)[0:1, :]
        pst_s[...] = bstart * rows
        bend = bstart + nblk
        j = _iota((nbp, V7X_LANES), 0).astype(F32)
        ln = _iota((nbp, V7X_LANES), 1)
        done = jnp.sum(jnp.where((ln < n_exp) & (bend <= j), 1.0, 0.0), axis=1, keepdims=True)
        blk = jnp.minimum(done, n_exp - 1.0)
        used = jnp.sum(jnp.where(ln == n_exp - 1, bend, 0.0), axis=1, keepdims=True)
        blk_ref[...] = jnp.where(ln == 0, blk, jnp.where(ln == 1, used, 0.0)).astype(I32)
        run_s[...] = jnp.zeros_like(run_s)

    @pl.when(ph == 1)
    def _():
        lower = jnp.where(_iota((tm, tm), 0) > _iota((tm, tm), 1), 1.0, 0.0).astype(BF16)
        excl = jnp.dot(lower, cnt.astype(BF16), preferred_element_type=F32)
        base = excl + run_s[...] + pst_s[...]
        d0 = jnp.sum(jnp.where(oh0, base, 0.0), axis=1, keepdims=True)
        d1 = jnp.sum(jnp.where(oh1, base, 0.0), axis=1, keepdims=True)
        dest_ref[...] = jnp.where(lane == 0, d0, jnp.where(lane == 1, d1, 0.0)).astype(I32)
        run_s[...] = run_s[...] + tile_cnt


def _dispatch(route_i, n_exp, rows, nbp, tm=512):
    M = route_i.shape[0]
    return pl.pallas_call(
        functools.partial(_dispatch_kernel, n_exp=n_exp, rows=rows, nbp=nbp),
        grid=(2, M // tm),
        in_specs=[pl.BlockSpec((tm, V7X_LANES), lambda p, i: (i, 0))],
        out_specs=[pl.BlockSpec((tm, V7X_LANES), lambda p, i: (i * p, 0)),
                   pl.BlockSpec((nbp, V7X_LANES), lambda p, i: (0, 0))],
        out_shape=[jax.ShapeDtypeStruct((M, V7X_LANES), I32),
                   jax.ShapeDtypeStruct((nbp, V7X_LANES), I32)],
        scratch_shapes=[pltpu.VMEM((1, V7X_LANES), F32), pltpu.VMEM((1, V7X_LANES), F32)],
        compiler_params=_cparams(("arbitrary", "arbitrary")),
        name="dispatch",
    )(route_i)


def _row_copy(src, dst, sem, r_src, r_dst):
    return pltpu.make_async_copy(src.at[pl.ds(r_src, 1), :], dst.at[pl.ds(r_dst, 1), :], sem)


def _scatter_kernel(dest_ref, x_ref, xs_in, xs_out, sem):
    del xs_in
    tm = x_ref.shape[0]
    base = pl.program_id(0) * (tm * TOP_K)

    def issue(r, _):
        for k in range(TOP_K):
            _row_copy(x_ref, xs_out, sem, r, dest_ref[base + r * TOP_K + k]).start()
        return 0

    lax.fori_loop(0, tm, issue, 0)

    def drain(r, _):
        for k in range(TOP_K):
            _row_copy(x_ref, xs_out, sem, 0, 0).wait()
        return 0

    lax.fori_loop(0, tm, drain, 0)


def _scatter(dest, x1, xs_zero, tm=256):
    M, D = x1.shape
    return pl.pallas_call(
        _scatter_kernel,
        grid_spec=pltpu.PrefetchScalarGridSpec(
            num_scalar_prefetch=1,
            grid=(M // tm,),
            in_specs=[pl.BlockSpec((tm, D), lambda i, d: (i, 0)),
                      pl.BlockSpec(memory_space=pl.ANY)],
            out_specs=pl.BlockSpec(memory_space=pl.ANY),
            scratch_shapes=[pltpu.SemaphoreType.DMA(())]),
        out_shape=jax.ShapeDtypeStruct(xs_zero.shape, xs_zero.dtype),
        input_output_aliases={2: 0},
        compiler_params=_cparams(("arbitrary",)),
        name="scatter",
    )(dest, x1, xs_zero)


def _experts_kernel(blk_ref, used_ref, xs_ref, wg_ref, wu_ref, wd_ref, o_ref):
    j = pl.program_id(0)

    @pl.when(j < used_ref[0])
    def _():
        xb = xs_ref[...].astype(BF16)
        g = jnp.dot(xb, wg_ref[...], preferred_element_type=F32)
        u = jnp.dot(xb, wu_ref[...], preferred_element_type=F32)
        hmid = (g * jax.nn.sigmoid(g) * u).astype(BF16)
        o_ref[...] = jnp.dot(hmid, wd_ref[...], preferred_element_type=F32)

    @pl.when(j >= used_ref[0])
    def _():
        o_ref[...] = jnp.zeros_like(o_ref)


def _experts(blk_e, used, xs, wg, wu, wd, rows):
    P, D = xs.shape
    De = wg.shape[2]
    nb = P // rows
    return pl.pallas_call(
        _experts_kernel,
        grid_spec=pltpu.PrefetchScalarGridSpec(
            num_scalar_prefetch=2,
            grid=(nb,),
            in_specs=[pl.BlockSpec((rows, D), lambda j, b, u: (j, 0)),
                      pl.BlockSpec((None, D, De), lambda j, b, u: (b[j], 0, 0)),
                      pl.BlockSpec((None, D, De), lambda j, b, u: (b[j], 0, 0)),
                      pl.BlockSpec((None, De, D), lambda j, b, u: (b[j], 0, 0))],
            out_specs=pl.BlockSpec((rows, D), lambda j, b, u: (j, 0))),
        out_shape=jax.ShapeDtypeStruct((P, D), F32),
        compiler_params=_cparams(("arbitrary",)),
        name="experts",
    )(blk_e, used, xs, wg, wu, wd)


def _post_kernel(dest_ref, x1_ref, rw_ref, p_ref, os_hbm, g2_ref, b2_ref, wpg_ref, bpg_ref, wpp_ref, pg_ref,
                 out_ref, gbuf, sem, *, alpha):
    tm = x1_ref.shape[0]
    base = pl.program_id(0) * (tm * TOP_K)

    def issue(r, _):
        for k in range(TOP_K):
            _row_copy(os_hbm, gbuf.at[k], sem, dest_ref[base + r * TOP_K + k], r).start()
        return 0

    lax.fori_loop(0, tm, issue, 0)

    def drain(r, _):
        for k in range(TOP_K):
            _row_copy(os_hbm, gbuf.at[k], sem, 0, 0).wait()
        return 0

    lax.fori_loop(0, tm, drain, 0)

    rw = rw_ref[...]
    ffn = rw[:, 0:1] * gbuf[0] + rw[:, 1:2] * gbuf[1]
    x2 = _layer_norm(alpha * x1_ref[...] + ffn, g2_ref[...], b2_ref[...])
    gate = jax.nn.sigmoid(jnp.dot(x2.astype(BF16), wpg_ref[...], preferred_element_type=F32) + bpg_ref[...])
    pp = jnp.dot(p_ref[...].astype(BF16), wpp_ref[...], preferred_element_type=F32)
    ple = pp * lax.rsqrt(jnp.mean(pp * pp, axis=-1, keepdims=True) + EPS) * pg_ref[...]
    out_ref[...] = x2 + gate * ple


def _post(dest, x1, route_w, p2d, os, g2, b2, wpg, bpg, wpp, pg, alpha, tm=256):
    M, D = x1.shape
    Dp = p2d.shape[1]
    tile = lambda w: pl.BlockSpec((tm, w), lambda i, d: (i, 0))
    full = lambda a: pl.BlockSpec(a.shape, lambda i, d: (0,) * a.ndim)
    return pl.pallas_call(
        functools.partial(_post_kernel, alpha=alpha),
        grid_spec=pltpu.PrefetchScalarGridSpec(
            num_scalar_prefetch=1,
            grid=(M // tm,),
            in_specs=[tile(D), tile(V7X_LANES), tile(Dp), pl.BlockSpec(memory_space=pl.ANY),
                      full(g2), full(b2), full(wpg), full(bpg), full(wpp), full(pg)],
            out_specs=tile(D),
            scratch_shapes=[pltpu.VMEM((TOP_K, tm, D), F32), pltpu.SemaphoreType.DMA(())]),
        out_shape=jax.ShapeDtypeStruct((M, D), F32),
        compiler_params=_cparams(("arbitrary",)),
        name="post",
    )(dest, x1, route_w, p2d, os, g2, b2, wpg, bpg, wpp, pg)


def kernel(x, p, w_in, conv_w, conv_b, w_q, w_k, b_i, b_f, mh_g, lam_re, lam_im, log_dt, b_re, b_im, c_re, c_im, d_skip, w_glu, b_glu, s5_g, w_out, ln1_g, ln1_b, w_grp, b_grp, w_rt, b_rt, w_eg, w_eu, w_ed, ln2_g, ln2_b, w_pg, b_pg, w_pp, ple_g):
    B, S, D = x.shape
    depth = w_in.shape[0]
    H, Dh = w_q.shape[1], w_q.shape[2]
    Dm = H * Dh
    G, Cn = d_skip.shape[1], d_skip.shape[2]
    Ds = G * Cn
    n_grp = w_grp.shape[-1]
    n_exp = w_eg.shape[1]
    epg = n_exp // n_grp
    M = B * S
    A = M * TOP_K
    alpha = (2 * depth) ** 0.25
    NC = S // MLSTM_CHUNK
    NS = S // S5_CHUNK
    nblocks = A // MOE_ROWS + n_exp
    nbp = -(-nblocks // V7X_SUBLANES) * V7X_SUBLANES
    row2 = lambda a: a.reshape(1, -1).astype(F32)

    xc = x.reshape(M, D).astype(F32)
    for l in range(depth):
        wi = w_in[l]
        w_main = jnp.concatenate([wi[:, :3 * Dm], wi[:, 3 * Dm + 2 * H:]], axis=1).astype(BF16)
        w_gate = jnp.pad(wi[:, 3 * Dm:3 * Dm + 2 * H], ((0, 0), (0, V7X_LANES - 2 * H))).astype(BF16)
        w_router = jnp.pad(jnp.concatenate([w_grp[l], w_rt[l]], axis=1).astype(F32),
                           ((0, 0), (0, V7X_LANES - n_grp - n_exp)))
        b_router = jnp.pad(jnp.concatenate([b_grp[l], b_rt[l]]).astype(F32),
                           (0, V7X_LANES - n_grp - n_exp)).reshape(1, -1)
        s5_ops = _s5_operators(lam_re[l], lam_im[l], log_dt[l], b_re[l], b_im[l], c_re[l], c_im[l],
                               d_skip[l], S5_CHUNK)

        z_main, z_gate = _inproj(xc, w_main, w_gate)

        gates = z_gate[:, :2 * H].reshape(B, NC, MLSTM_CHUNK, 2, H).transpose(0, 4, 3, 1, 2)
        y_m = _mlstm(z_main.reshape(B, S, -1), gates, jnp.stack([b_i[l], b_f[l]]).astype(F32),
                     conv_w[l].astype(F32), row2(conv_b[l]), w_q[l].astype(BF16), w_k[l].astype(BF16),
                     row2(mh_g[l]), H, Dh).reshape(M, Dm)

        ug = (z_main[:, 3 * Dm:].reshape(B, NS, S5_CHUNK, G, Cn).transpose(3, 1, 0, 2, 4)
              .reshape(G, NS * B, S5_CHUNK * Cn))
        yg = _s5(ug, *s5_ops, nb=B)
        y_s = yg.reshape(G, NS, B, S5_CHUNK, Cn).transpose(2, 1, 3, 0, 4).reshape(M, Ds)

        wo = w_out[l].astype(BF16)
        x1, route_i, route_w = _mixout(y_m, y_s, xc, w_glu[l].astype(BF16), row2(b_glu[l]), row2(s5_g[l]),
                                       wo[:Dm], wo[Dm:], row2(ln1_g[l]), row2(ln1_b[l]),
                                       w_router, b_router, alpha, n_grp, epg)

        dest_l, blk_l = _dispatch(route_i, n_exp, MOE_ROWS, nbp)
        dest = dest_l[:, :TOP_K].reshape(A)
        xs = _scatter(dest, x1, jnp.zeros((nblocks * MOE_ROWS, D), F32))
        os = _experts(blk_l[:nblocks, 0], blk_l[:1, 1], xs,
                      w_eg[l].astype(BF16), w_eu[l].astype(BF16), w_ed[l].astype(BF16), MOE_ROWS)
        xc = _post(dest, x1, route_w, p[l].reshape(M, -1), os, row2(ln2_g[l]), row2(ln2_b[l]),
                   w_pg[l].astype(BF16), row2(b_pg[l]), w_pp[l].astype(BF16), row2(ple_g[l]), alpha)
    return xc.reshape(B, S, D).astype(x.dtype)
```

```python
import functools
import math

import jax
import jax.numpy as jnp
from jax import lax
from jax.experimental import pallas as pl
from jax.experimental.pallas import tpu as pltpu

F32 = jnp.float32
BF16 = jnp.bfloat16
I32 = jnp.int32
HIGHEST = lax.Precision.HIGHEST

EPS = 1e-5
V7X_LANES = 128
V7X_SUBLANES = 8
VMEM_LIMIT = 56 * 1024 * 1024

MLSTM_CHUNK = 256
S5_CHUNK = 16
MOE_ROWS = 256
TOP_K = 2


def _cparams(sem):
    return pltpu.CompilerParams(dimension_semantics=sem, vmem_limit_bytes=VMEM_LIMIT)


def _iota(shape, axis):
    return lax.broadcasted_iota(I32, shape, axis)


def _inproj_kernel(x_ref, wm_ref, wg_ref, zm_ref, zg_ref):
    xb = x_ref[...].astype(BF16)
    zm_ref[...] = jnp.dot(xb, wm_ref[...], preferred_element_type=F32)
    zg_ref[...] = jnp.dot(xb, wg_ref[...], preferred_element_type=F32)


def _inproj(x2d, w_main, w_gate, tm=512):
    M, D = x2d.shape
    N = w_main.shape[1]
    return pl.pallas_call(
        _inproj_kernel,
        grid=(M // tm,),
        in_specs=[pl.BlockSpec((tm, D), lambda i: (i, 0)),
                  pl.BlockSpec((D, N), lambda i: (0, 0)),
                  pl.BlockSpec((D, V7X_LANES), lambda i: (0, 0))],
        out_specs=[pl.BlockSpec((tm, N), lambda i: (i, 0)),
                   pl.BlockSpec((tm, V7X_LANES), lambda i: (i, 0))],
        out_shape=[jax.ShapeDtypeStruct((M, N), F32),
                   jax.ShapeDtypeStruct((M, V7X_LANES), F32)],
        compiler_params=_cparams(("parallel",)),
        name="inproj",
    )(x2d, w_main, w_gate)


def _mlstm_kernel(bif_ref, um_ref, v_ref, o_ref, gt_ref, cw_ref, cb_ref, wq_ref, wk_ref, mhg_ref,
                  y_ref, q_s, k_s, bcum_s, an_s, r_s, g_s, ml_s):
    h = pl.program_id(1)
    S, Dh = um_ref.shape
    NC, L = gt_ref.shape[1], gt_ref.shape[2]

    u = um_ref[...]
    row = _iota((S, Dh), 0)
    width = cw_ref.shape[0]
    acc = cb_ref[...] + cw_ref[width - 1:width, :] * u
    for j in range(width - 1):
        sh = width - 1 - j
        us = jnp.where(row >= sh, pltpu.roll(u, sh, axis=0), 0.0)
        acc = acc + cw_ref[j:j + 1, :] * us
    cb = (acc * jax.nn.sigmoid(acc)).astype(BF16)
    q_s[...] = jnp.dot(cb, wq_ref[...], preferred_element_type=F32)
    k_s[...] = jnp.dot(cb, wk_ref[...], preferred_element_type=F32) * (Dh ** -0.5)

    ig = gt_ref[0] + bif_ref[0, h]
    fp = gt_ref[1] + bif_ref[1, h]
    lf = jnp.minimum(fp, 0.0) - jnp.log1p(jnp.exp(-jnp.abs(fp)))
    tri = jnp.where(_iota((L, L), 0) <= _iota((L, L), 1), 1.0, 0.0).astype(F32)
    bcum = jnp.dot(lf, tri, precision=HIGHEST, preferred_element_type=F32)
    g = bcum[:, L - 1:L]
    a = g - bcum + ig
    m_loc = jnp.max(a, axis=-1, keepdims=True)
    bcum_s[...] = bcum
    an_s[...] = a - m_loc
    r_s[...] = ig - bcum
    g_s[...] = jnp.broadcast_to(g, (NC, V7X_LANES))
    ml_s[...] = jnp.broadcast_to(m_loc, (NC, V7X_LANES))

    eye = _iota((L, L), 0) == _iota((L, L), 1)
    causal = _iota((L, L), 0) >= _iota((L, L), 1)

    def to_col(row_vec):
        return jnp.sum(jnp.where(eye, row_vec, 0.0), axis=1, keepdims=True)

    def chunk(c, carry):
        C, n, m = carry
        t0 = pl.multiple_of(c * L, L)
        b_col = to_col(bcum_s[pl.ds(c, 1), :])
        an_col = to_col(an_s[pl.ds(c, 1), :])
        r_row = r_s[pl.ds(c, 1), :]
        g_c = g_s[pl.ds(c, 1), :][:, 0:1]
        ml_c = ml_s[pl.ds(c, 1), :][:, 0:1]
        qc = q_s[pl.ds(t0, L), :]
        kc = k_s[pl.ds(t0, L), :]
        vb = v_ref[pl.ds(t0, L), :].astype(BF16)
        qb = qc.astype(BF16)

        log_d = jnp.where(causal, b_col + r_row, -jnp.inf)
        log_inter = b_col + m
        m_t = jnp.maximum(log_inter, jnp.max(log_d, axis=1, keepdims=True))
        s = lax.dot_general(qb, kc.astype(BF16), (((1,), (1,)), ((), ())),
                            preferred_element_type=F32) * jnp.exp(log_d - m_t)
        inter = jnp.exp(log_inter - m_t)
        num = (jnp.dot(s.astype(BF16), vb, preferred_element_type=F32)
               + inter * jnp.dot(qb, C.astype(BF16), preferred_element_type=F32))
        den = jnp.sum(s, axis=1, keepdims=True) + inter * jnp.sum(qc * n, axis=1, keepdims=True)
        den = jnp.maximum(jnp.abs(den), jnp.exp(-m_t))
        hh = jax.nn.sigmoid(o_ref[pl.ds(t0, L), :]) * (num / den)
        mu = jnp.mean(hh, axis=-1, keepdims=True)
        hc = hh - mu
        var = jnp.mean(hc * hc, axis=-1, keepdims=True)
        y_ref[pl.ds(t0, L), :] = hc * lax.rsqrt(var + EPS) * mhg_ref[...]

        wk = jnp.exp(an_col) * kc
        Cc = lax.dot_general(wk.astype(BF16), vb, (((0,), (0,)), ((), ())), preferred_element_type=F32)
        nc = jnp.sum(wk, axis=0, keepdims=True)
        m_new = jnp.maximum(g_c + m, ml_c)
        s_old = jnp.exp(g_c + m - m_new)
        s_new = jnp.exp(ml_c - m_new)
        return s_old * C + s_new * Cc, s_old * n + s_new * nc, m_new

    init = (jnp.zeros((Dh, Dh), F32), jnp.zeros((1, Dh), F32), jnp.zeros((1, 1), F32))
    lax.fori_loop(0, NC, chunk, init, unroll=2)


def _mlstm(z3, gates, b_if, conv_w, conv_b, wq, wk, mh_g, H, Dh):
    B, S, _ = z3.shape
    NC, L = gates.shape[3], gates.shape[4]
    blk = lambda off: pl.BlockSpec((None, S, Dh), lambda b, h: (b, 0, off + h))
    return pl.pallas_call(
        _mlstm_kernel,
        grid=(B, H),
        in_specs=[pl.BlockSpec(memory_space=pltpu.SMEM),
                  blk(0), blk(H), blk(2 * H),
                  pl.BlockSpec((None, None, 2, NC, L), lambda b, h: (b, h, 0, 0, 0)),
                  pl.BlockSpec((conv_w.shape[0], Dh), lambda b, h: (0, h)),
                  pl.BlockSpec((1, Dh), lambda b, h: (0, h)),
                  pl.BlockSpec((None, Dh, Dh), lambda b, h: (h, 0, 0)),
                  pl.BlockSpec((None, Dh, Dh), lambda b, h: (h, 0, 0)),
                  pl.BlockSpec((1, Dh), lambda b, h: (0, h))],
        out_specs=pl.BlockSpec((None, S, Dh), lambda b, h: (b, 0, h)),
        out_shape=jax.ShapeDtypeStruct((B, S, H * Dh), F32),
        scratch_shapes=[pltpu.VMEM((S, Dh), F32), pltpu.VMEM((S, Dh), F32),
                        pltpu.VMEM((NC, L), F32), pltpu.VMEM((NC, L), F32), pltpu.VMEM((NC, L), F32),
                        pltpu.VMEM((NC, V7X_LANES), F32), pltpu.VMEM((NC, V7X_LANES), F32)],
        compiler_params=_cparams(("parallel", "parallel")),
        name="mlstm",
    )(b_if, z3, z3, z3, gates, conv_w, conv_b, wq, wk, mh_g)


def _s5_operators(lam_re, lam_im, log_dt, b_re, b_im, c_re, c_im, d_skip, Lc):
    G, P = lam_re.shape
    Cn = b_re.shape[-1]
    lr, li = lam_re.astype(F32), lam_im.astype(F32)
    dt = jnp.exp(log_dt.astype(F32))[:, None]
    er = jnp.exp(lr * dt)
    ar, ai = er * jnp.cos(li * dt), er * jnp.sin(li * dt)
    mag2 = lr * lr + li * li
    xr, xi = ar - 1.0, ai
    cr = (xr * lr + xi * li) / mag2
    ci = (xi * lr - xr * li) / mag2
    bbr = cr[..., None] * b_re - ci[..., None] * b_im
    bbi = cr[..., None] * b_im + ci[..., None] * b_re
    pr, pi = [jnp.ones_like(ar)], [jnp.zeros_like(ai)]
    for _ in range(Lc):
        pr.append(pr[-1] * ar - pi[-1] * ai)
        pi.append(pr[-2] * ai + pi[-1] * ar)
    pwr, pwi = jnp.stack(pr, -1), jnp.stack(pi, -1)

    rev_r = pwr[:, :, Lc - 1::-1] if Lc > 1 else pwr[:, :, :1]
    rev_i = pwi[:, :, Lc - 1::-1] if Lc > 1 else pwi[:, :, :1]
    wr = jnp.einsum('gpt,gpc->gtcp', rev_r, bbr) - jnp.einsum('gpt,gpc->gtcp', rev_i, bbi)
    wi = jnp.einsum('gpt,gpc->gtcp', rev_r, bbi) + jnp.einsum('gpt,gpc->gtcp', rev_i, bbr)
    w2 = jnp.concatenate([wr, wi, wi, wr], axis=-1).reshape(G, Lc * Cn, 4 * P)

    car = c_re[..., None] * pwr[:, None] - c_im[..., None] * pwi[:, None]
    cai = c_re[..., None] * pwi[:, None] + c_im[..., None] * pwr[:, None]
    v_re = car[..., 1:].transpose(0, 2, 3, 1)
    v_im = -cai[..., 1:].transpose(0, 2, 3, 1)
    vmat = jnp.concatenate([v_re, v_im], axis=1).reshape(G, 2 * P, Lc * Cn)

    kern = (jnp.einsum('gcpk,gpd->gkdc', car[..., :Lc], bbr, precision=HIGHEST)
            - jnp.einsum('gcpk,gpd->gkdc', cai[..., :Lc], bbi, precision=HIGHEST))
    lag = jnp.arange(Lc)[None, :] - jnp.arange(Lc)[:, None]
    tm = jnp.where((lag >= 0)[None, :, :, None, None], kern[:, jnp.clip(lag, 0, Lc - 1)], 0.0)
    tmat = tm.transpose(0, 1, 3, 2, 4).reshape(G, Lc * Cn, Lc * Cn)

    pL, qL = pwr[..., Lc], pwi[..., Lc]
    pq = jnp.stack([jnp.concatenate([pL, pL], -1), jnp.concatenate([-qL, qL], -1),
                    jnp.concatenate([qL, -qL], -1)], axis=1)
    pq = jnp.concatenate([pq, jnp.zeros((G, V7X_SUBLANES - 3, 2 * P), F32)], axis=1)
    dtile = jnp.tile(d_skip.astype(F32), (1, Lc))[:, None, :]
    return w2.astype(BF16), tmat.astype(BF16), vmat.astype(BF16), pq, dtile


def _s5_kernel(u_ref, w2_ref, t_ref, v_ref, pq_ref, d_ref, y_ref, z_s, xp_s, *, nb):
    R = u_ref.shape[0]
    P2 = xp_s.shape[1]
    u = u_ref[...]
    ub = u.astype(BF16)
    z_s[...] = jnp.dot(ub, w2_ref[...], preferred_element_type=F32)
    pv, qv, q2 = pq_ref[0:1, :], pq_ref[1:2, :], pq_ref[2:3, :]

    def step(c, carry):
        x, xs = carry
        r0 = pl.multiple_of(c * nb, nb)
        xp_s[pl.ds(r0, nb), :] = x
        z = z_s[pl.ds(r0, nb), :]
        return x * pv + xs * qv + z[:, :P2], xs * pv + x * q2 + z[:, P2:]

    zero = jnp.zeros((nb, P2), F32)
    lax.fori_loop(0, R // nb, step, (zero, zero), unroll=8)
    y_ref[...] = (jnp.dot(ub, t_ref[...], preferred_element_type=F32)
                  + jnp.dot(xp_s[...].astype(BF16), v_ref[...], preferred_element_type=F32)
                  + u * d_ref[...])


def _s5(ug, w2, tmat, vmat, pq, dtile, nb):
    G, R, W = ug.shape
    P2 = vmat.shape[1]
    sq = lambda *shape: pl.BlockSpec((None,) + shape, lambda g: (g,) + (0,) * len(shape))
    return pl.pallas_call(
        functools.partial(_s5_kernel, nb=nb),
        grid=(G,),
        in_specs=[sq(R, W), sq(W, 2 * P2), sq(W, W), sq(P2, W), sq(V7X_SUBLANES, P2), sq(1, W)],
        out_specs=sq(R, W),
        out_shape=jax.ShapeDtypeStruct((G, R, W), F32),
        scratch_shapes=[pltpu.VMEM((R, 2 * P2), F32), pltpu.VMEM((R, P2), F32)],
        compiler_params=_cparams(("parallel",)),
        name="s5",
    )(ug, w2, tmat, vmat, pq, dtile)


def _layer_norm(v, g, b):
    mu = jnp.mean(v, axis=-1, keepdims=True)
    vc = v - mu
    var = jnp.mean(vc * vc, axis=-1, keepdims=True)
    return vc * lax.rsqrt(var + EPS) * g + b


def _mixout_kernel(ym_ref, ys_ref, x_ref, wglu_ref, bglu_ref, s5g_ref, wom_ref, wos_ref,
                   g1_ref, b1_ref, wrh_ref, wrl_ref, br_ref, x1_ref, ri_ref, rw_ref, *, alpha, n_grp, epg):
    ys = ys_ref[...]
    gy = 0.5 * ys * (1.0 + jnp.tanh(math.sqrt(2.0 / math.pi) * (ys + 0.044715 * (ys * ys * ys))))
    glu = gy * jax.nn.sigmoid(jnp.dot(gy.astype(BF16), wglu_ref[...], preferred_element_type=F32)
                              + bglu_ref[...])
    ysn = glu * lax.rsqrt(jnp.mean(glu * glu, axis=-1, keepdims=True) + EPS) * s5g_ref[...]
    mix = (jnp.dot(ym_ref[...].astype(BF16), wom_ref[...], preferred_element_type=F32)
           + jnp.dot(ysn.astype(BF16), wos_ref[...], preferred_element_type=F32))
    x1 = _layer_norm(alpha * x_ref[...] + mix, g1_ref[...], b1_ref[...])
    x1_ref[...] = x1

    xh = x1.astype(BF16)
    xl = (x1 - xh.astype(F32)).astype(BF16)
    logits = (jnp.dot(xh, wrh_ref[...], preferred_element_type=F32)
              + jnp.dot(xl, wrh_ref[...], preferred_element_type=F32)
              + jnp.dot(xh, wrl_ref[...], preferred_element_type=F32) + br_ref[...])
    tm = logits.shape[0]
    lane = _iota((tm, V7X_LANES), 1)
    big = jnp.int32(V7X_LANES)
    neg = -jnp.inf
    gl = jnp.where(lane < n_grp, logits, neg)
    gmax = jnp.max(gl, axis=-1, keepdims=True)
    grp = jnp.min(jnp.where(gl == gmax, lane, big), axis=-1, keepdims=True)
    p_grp = 1.0 / jnp.sum(jnp.exp(gl - gmax), axis=-1, keepdims=True)
    lo = n_grp + grp * epg
    el = jnp.where((lane >= lo) & (lane < lo + epg), logits, neg)
    v1 = jnp.max(el, axis=-1, keepdims=True)
    i1 = jnp.min(jnp.where(el == v1, lane, big), axis=-1, keepdims=True)
    el2 = jnp.where(lane == i1, neg, el)
    v2 = jnp.max(el2, axis=-1, keepdims=True)
    i2 = jnp.min(jnp.where(el2 == v2, lane, big), axis=-1, keepdims=True)
    e2 = jnp.exp(v2 - v1)
    w1 = p_grp / (1.0 + e2)
    w2 = p_grp * e2 / (1.0 + e2)
    ri_ref[...] = jnp.where(lane == 0, i1 - n_grp, jnp.where(lane == 1, i2 - n_grp, 0))
    rw_ref[...] = jnp.where(lane == 0, w1, jnp.where(lane == 1, w2, 0.0))


def _mixout(ym, ys, x2d, wglu, bglu, s5g, wom, wos, g1, b1, wrh, wrl, br, alpha, n_grp, epg, tm=512):
    M, D = x2d.shape
    Dm, Ds = ym.shape[1], ys.shape[1]
    tile = lambda w: pl.BlockSpec((tm, w), lambda i: (i, 0))
    full = lambda a: pl.BlockSpec(a.shape, lambda i: (0,) * a.ndim)
    return pl.pallas_call(
        functools.partial(_mixout_kernel, alpha=alpha, n_grp=n_grp, epg=epg),
        grid=(M // tm,),
        in_specs=[tile(Dm), tile(Ds), tile(D), full(wglu), full(bglu), full(s5g), full(wom), full(wos),
                  full(g1), full(b1), full(wrh), full(wrl), full(br)],
        out_specs=[tile(D), tile(V7X_LANES), tile(V7X_LANES)],
        out_shape=[jax.ShapeDtypeStruct((M, D), F32),
                   jax.ShapeDtypeStruct((M, V7X_LANES), I32),
                   jax.ShapeDtypeStruct((M, V7X_LANES), F32)],
        compiler_params=_cparams(("parallel",)),
        name="mixout",
    )(ym, ys, x2d, wglu, bglu, s5g, wom, wos, g1, b1, wrh, wrl, br)


def _dispatch_kernel(ri_ref, dest_ref, blk_ref, run_s, pst_s, *, n_exp, rows, nbp):
    ph = pl.program_id(0)
    i = pl.program_id(1)
    tm = ri_ref.shape[0]
    lane = _iota((tm, V7X_LANES), 1)
    oh0 = lane == ri_ref[:, 0:1]
    oh1 = lane == ri_ref[:, 1:2]
    cnt = jnp.where(oh0 | oh1, 1.0, 0.0).astype(F32)
    tile_cnt = jnp.sum(cnt, axis=0, keepdims=True)

    @pl.when((ph == 0) & (i == 0))
    def _():
        run_s[...] = jnp.zeros_like(run_s)

    @pl.when(ph == 0)
    def _():
        run_s[...] = run_s[...] + tile_cnt
        dest_ref[...] = jnp.zeros_like(dest_ref)

    @pl.when((ph == 1) & (i == 0))
    def _():
        counts = run_s[...]
        nblk = jnp.floor((counts + (rows - 1)) / rows)
        tri = jnp.where(_iota((V7X_LANES, V7X_LANES), 0) < _iota((V7X_LANES, V7X_LANES), 1), 1.0, 0.0)
        bstart = jnp.dot(jnp.broadcast_to(nblk, (V7X_SUBLANES, V7X_LANES)), tri.astype(F32),
                         precision=HIGHEST, preferred_element_type=F32)[0:1, :]
        pst_s[...] = bstart * rows
        bend = bstart + nblk
        j = _iota((nbp, V7X_LANES), 0).astype(F32)
        ln = _iota((nbp, V7X_LANES), 1)
        done = jnp.sum(jnp.where((ln < n_exp) & (bend <= j), 1.0, 0.0), axis=1, keepdims=True)
        blk = jnp.minimum(done, n_exp - 1.0)
        used = jnp.sum(jnp.where(ln == n_exp - 1, bend, 0.0), axis=1, keepdims=True)
        blk_ref[...] = jnp.where(ln == 0, blk, jnp.where(ln == 1, used, 0.0)).astype(I32)
        run_s[...] = jnp.zeros_like(run_s)

    @pl.when(ph == 1)
    def _():
        lower = jnp.where(_iota((tm, tm), 0) > _iota((tm, tm), 1), 1.0, 0.0).astype(BF16)
        excl = jnp.dot(lower, cnt.astype(BF16), preferred_element_type=F32)
        base = excl + run_s[...] + pst_s[...]
        d0 = jnp.sum(jnp.where(oh0, base, 0.0), axis=1, keepdims=True)
        d1 = jnp.sum(jnp.where(oh1, base, 0.0), axis=1, keepdims=True)
        dest_ref[...] = jnp.where(lane == 0, d0, jnp.where(lane == 1, d1, 0.0)).astype(I32)
        run_s[...] = run_s[...] + tile_cnt


def _dispatch(route_i, n_exp, rows, nbp, tm=512):
    M = route_i.shape[0]
    return pl.pallas_call(
        functools.partial(_dispatch_kernel, n_exp=n_exp, rows=rows, nbp=nbp),
        grid=(2, M // tm),
        in_specs=[pl.BlockSpec((tm, V7X_LANES), lambda p, i: (i, 0))],
        out_specs=[pl.BlockSpec((tm, V7X_LANES), lambda p, i: (i * p, 0)),
                   pl.BlockSpec((nbp, V7X_LANES), lambda p, i: (0, 0))],
        out_shape=[jax.ShapeDtypeStruct((M, V7X_LANES), I32),
                   jax.ShapeDtypeStruct((nbp, V7X_LANES), I32)],
        scratch_shapes=[pltpu.VMEM((1, V7X_LANES), F32), pltpu.VMEM((1, V7X_LANES), F32)],
        compiler_params=_cparams(("arbitrary", "arbitrary")),
        name="dispatch",
    )(route_i)


def _row_copy(src, dst, sem, r_src, r_dst):
    return pltpu.make_async_copy(src.at[pl.ds(r_src, 1), :], dst.at[pl.ds(r_dst, 1), :], sem)


def _scatter_kernel(dest_ref, x_ref, xs_in, xs_out, sem):
    del xs_in
    tm = x_ref.shape[0]
    base = pl.program_id(0) * (tm * TOP_K)

    def issue(r, _):
        for k in range(TOP_K):
            _row_copy(x_ref, xs_out, sem, r, dest_ref[base + r * TOP_K + k]).start()
        return 0

    lax.fori_loop(0, tm, issue, 0, unroll=8)

    def drain(r, _):
        for k in range(TOP_K):
            _row_copy(x_ref, xs_out, sem, 0, 0).wait()
        return 0

    lax.fori_loop(0, tm, drain, 0, unroll=8)


def _scatter(dest, x1, xs_zero, tm=256):
    M, D = x1.shape
    return pl.pallas_call(
        _scatter_kernel,
        grid_spec=pltpu.PrefetchScalarGridSpec(
            num_scalar_prefetch=1,
            grid=(M // tm,),
            in_specs=[pl.BlockSpec((tm, D), lambda i, d: (i, 0)),
                      pl.BlockSpec(memory_space=pl.ANY)],
            out_specs=pl.BlockSpec(memory_space=pl.ANY),
            scratch_shapes=[pltpu.SemaphoreType.DMA(())]),
        out_shape=jax.ShapeDtypeStruct(xs_zero.shape, xs_zero.dtype),
        input_output_aliases={2: 0},
        compiler_params=_cparams(("arbitrary",)),
        name="scatter",
    )(dest, x1, xs_zero)


def _experts_kernel(blk_ref, used_ref, xs_ref, wg_ref, wu_ref, wd_ref, o_ref, wg_s, wu_s, wd_s):
    j = pl.program_id(0)
    active = j < used_ref[0]

    @pl.when(active & ((j == 0) | (blk_ref[j] != blk_ref[jnp.maximum(j - 1, 0)])))
    def _():
        wg_s[...] = wg_ref[...].astype(BF16)
        wu_s[...] = wu_ref[...].astype(BF16)
        wd_s[...] = wd_ref[...].astype(BF16)

    @pl.when(active)
    def _():
        xb = xs_ref[...].astype(BF16)
        g = jnp.dot(xb, wg_s[...], preferred_element_type=F32)
        u = jnp.dot(xb, wu_s[...], preferred_element_type=F32)
        hmid = (g * jax.nn.sigmoid(g) * u).astype(BF16)
        o_ref[...] = jnp.dot(hmid, wd_s[...], preferred_element_type=F32)

    @pl.when(j >= used_ref[0])
    def _():
        o_ref[...] = jnp.zeros_like(o_ref)


def _experts(blk_e, used, xs, wg, wu, wd, layer, rows):
    P, D = xs.shape
    De = wg.shape[3]
    nb = P // rows
    wspec = lambda a, b_: pl.BlockSpec((None, None, a, b_), lambda j, b, u: (layer, b[j], 0, 0))
    return pl.pallas_call(
        _experts_kernel,
        grid_spec=pltpu.PrefetchScalarGridSpec(
            num_scalar_prefetch=2,
            grid=(nb,),
            in_specs=[pl.BlockSpec((rows, D), lambda j, b, u: (j, 0)),
                      wspec(D, De), wspec(D, De), wspec(De, D)],
            out_specs=pl.BlockSpec((rows, D), lambda j, b, u: (j, 0)),
            scratch_shapes=[pltpu.VMEM((D, De), BF16), pltpu.VMEM((D, De), BF16), pltpu.VMEM((De, D), BF16)]),
        out_shape=jax.ShapeDtypeStruct((P, D), F32),
        compiler_params=_cparams(("arbitrary",)),
        name="experts",
    )(blk_e, used, xs, wg, wu, wd)


def _post_kernel(dest_ref, x1_ref, rw_ref, p_ref, os_hbm, g2_ref, b2_ref, wpg_ref, bpg_ref, wpp_ref, pg_ref,
                 out_ref, gbuf, sem, *, alpha):
    i = pl.program_id(0)
    tm = x1_ref.shape[0]
    slot = lax.rem(i, 2)

    def start_gather(tile, slot_):
        base = tile * (tm * TOP_K)

        def issue(r, _):
            for k in range(TOP_K):
                _row_copy(os_hbm, gbuf.at[slot_, k], sem.at[slot_], dest_ref[base + r * TOP_K + k], r).start()
            return 0

        lax.fori_loop(0, tm, issue, 0, unroll=8)

    @pl.when(i == 0)
    def _():
        start_gather(0, 0)

    @pl.when(i + 1 < pl.num_programs(0))
    def _():
        start_gather(i + 1, 1 - slot)

    def drain(r, _):
        for k in range(TOP_K):
            _row_copy(os_hbm, gbuf.at[slot, k], sem.at[slot], 0, 0).wait()
        return 0

    lax.fori_loop(0, tm, drain, 0, unroll=8)

    rw = rw_ref[...]
    ffn = rw[:, 0:1] * gbuf[slot, 0] + rw[:, 1:2] * gbuf[slot, 1]
    x2 = _layer_norm(alpha * x1_ref[...] + ffn, g2_ref[...], b2_ref[...])
    gate = jax.nn.sigmoid(jnp.dot(x2.astype(BF16), wpg_ref[...], preferred_element_type=F32) + bpg_ref[...])
    pp = jnp.dot(p_ref[...].astype(BF16), wpp_ref[...], preferred_element_type=F32)
    ple = pp * lax.rsqrt(jnp.mean(pp * pp, axis=-1, keepdims=True) + EPS) * pg_ref[...]
    out_ref[...] = x2 + gate * ple


def _post(dest, x1, route_w, p3d, layer, os, g2, b2, wpg, bpg, wpp, pg, alpha, tm=256):
    M, D = x1.shape
    Dp = p3d.shape[2]
    tile = lambda w: pl.BlockSpec((tm, w), lambda i, d: (i, 0))
    full = lambda a: pl.BlockSpec(a.shape, lambda i, d: (0,) * a.ndim)
    return pl.pallas_call(
        functools.partial(_post_kernel, alpha=alpha),
        grid_spec=pltpu.PrefetchScalarGridSpec(
            num_scalar_prefetch=1,
            grid=(M // tm,),
            in_specs=[tile(D), tile(V7X_LANES), pl.BlockSpec((None, tm, Dp), lambda i, d: (layer, i, 0)),
                      pl.BlockSpec(memory_space=pl.ANY),
                      full(g2), full(b2), full(wpg), full(bpg), full(wpp), full(pg)],
            out_specs=tile(D),
            scratch_shapes=[pltpu.VMEM((2, TOP_K, tm, D), F32), pltpu.SemaphoreType.DMA((2,))]),
        out_shape=jax.ShapeDtypeStruct((M, D), F32),
        compiler_params=_cparams(("arbitrary",)),
        name="post",
    )(dest, x1, route_w, p3d, os, g2, b2, wpg, bpg, wpp, pg)


def kernel(x, p, w_in, conv_w, conv_b, w_q, w_k, b_i, b_f, mh_g, lam_re, lam_im, log_dt, b_re, b_im, c_re, c_im, d_skip, w_glu, b_glu, s5_g, w_out, ln1_g, ln1_b, w_grp, b_grp, w_rt, b_rt, w_eg, w_eu, w_ed, ln2_g, ln2_b, w_pg, b_pg, w_pp, ple_g):
    B, S, D = x.shape
    depth = w_in.shape[0]
    H, Dh = w_q.shape[1], w_q.shape[2]
    Dm = H * Dh
    G, Cn = d_skip.shape[1], d_skip.shape[2]
    Ds = G * Cn
    n_grp = w_grp.shape[-1]
    n_exp = w_eg.shape[1]
    epg = n_exp // n_grp
    M = B * S
    A = M * TOP_K
    alpha = (2 * depth) ** 0.25
    NC = S // MLSTM_CHUNK
    NS = S // S5_CHUNK
    nblocks = A // MOE_ROWS + n_exp
    nbp = -(-nblocks // V7X_SUBLANES) * V7X_SUBLANES
    row2 = lambda a: a.reshape(1, -1).astype(F32)

    xc = x.reshape(M, D).astype(F32)
    for l in range(depth):
        wi = w_in[l]
        w_main = jnp.concatenate([wi[:, :3 * Dm], wi[:, 3 * Dm + 2 * H:]], axis=1).astype(BF16)
        w_gate = jnp.pad(wi[:, 3 * Dm:3 * Dm + 2 * H], ((0, 0), (0, V7X_LANES - 2 * H))).astype(BF16)
        w_router = jnp.pad(jnp.concatenate([w_grp[l], w_rt[l]], axis=1).astype(F32),
                           ((0, 0), (0, V7X_LANES - n_grp - n_exp)))
        w_router_hi = w_router.astype(BF16)
        w_router_lo = (w_router - w_router_hi.astype(F32)).astype(BF16)
        b_router = jnp.pad(jnp.concatenate([b_grp[l], b_rt[l]]).astype(F32),
                           (0, V7X_LANES - n_grp - n_exp)).reshape(1, -1)
        s5_ops = _s5_operators(lam_re[l], lam_im[l], log_dt[l], b_re[l], b_im[l], c_re[l], c_im[l],
                               d_skip[l], S5_CHUNK)

        z_main, z_gate = _inproj(xc, w_main, w_gate)

        gates = z_gate[:, :2 * H].reshape(B, NC, MLSTM_CHUNK, 2, H).transpose(0, 4, 3, 1, 2)
        y_m = _mlstm(z_main.reshape(B, S, -1), gates, jnp.stack([b_i[l], b_f[l]]).astype(F32),
                     conv_w[l].astype(F32), row2(conv_b[l]), w_q[l].astype(BF16), w_k[l].astype(BF16),
                     row2(mh_g[l]), H, Dh).reshape(M, Dm)

        ug = (z_main[:, 3 * Dm:].reshape(B, NS, S5_CHUNK, G, Cn).transpose(3, 1, 0, 2, 4)
              .reshape(G, NS * B, S5_CHUNK * Cn))
        yg = _s5(ug, *s5_ops, nb=B)
        y_s = yg.reshape(G, NS, B, S5_CHUNK, Cn).transpose(2, 1, 3, 0, 4).reshape(M, Ds)

        wo = w_out[l].astype(BF16)
        x1, route_i, route_w = _mixout(y_m, y_s, xc, w_glu[l].astype(BF16), row2(b_glu[l]), row2(s5_g[l]),
                                       wo[:Dm], wo[Dm:], row2(ln1_g[l]), row2(ln1_b[l]),
                                       w_router_hi, w_router_lo, b_router, alpha, n_grp, epg)

        dest_l, blk_l = _dispatch(route_i, n_exp, MOE_ROWS, nbp)
        dest = dest_l[:, :TOP_K].reshape(A)
        xs = _scatter(dest, x1, jnp.zeros((nblocks * MOE_ROWS, D), F32))
        os = _experts(blk_l[:nblocks, 0], blk_l[:1, 1], xs,
                      w_eg, w_eu, w_ed, l, MOE_ROWS)
        xc = _post(dest, x1, route_w, p.reshape(depth, M, -1), l, os, row2(ln2_g[l]), row2(ln2_b[l]),
                   w_pg[l].astype(BF16), row2(b_pg[l]), w_pp[l].astype(BF16), row2(ple_g[l]), alpha)
    return xc.reshape(B, S, D).astype(x.dtype)
```

```python
import functools
import math

import jax
import jax.numpy as jnp
from jax import lax
from jax.experimental import pallas as pl
from jax.experimental.pallas import tpu as pltpu

F32 = jnp.float32
BF16 = jnp.bfloat16
I32 = jnp.int32
HIGHEST = lax.Precision.HIGHEST

EPS = 1e-5
V7X_LANES = 128
V7X_SUBLANES = 8
VMEM_LIMIT = 56 * 1024 * 1024

MLSTM_CHUNK = 256
S5_CHUNK = 16
MOE_ROWS = 256
TOP_K = 2


def _cparams(sem):
    return pltpu.CompilerParams(dimension_semantics=sem, vmem_limit_bytes=VMEM_LIMIT)


def _iota(shape, axis):
    return lax.broadcasted_iota(I32, shape, axis)


def _inproj_kernel(x_ref, wm_ref, wg_ref, zm_ref, zg_ref):
    xb = x_ref[...].astype(BF16)
    zm_ref[...] = jnp.dot(xb, wm_ref[...], preferred_element_type=F32)
    zg_ref[...] = jnp.dot(xb, wg_ref[...], preferred_element_type=F32)


def _inproj(x2d, w_main, w_gate, tm=512):
    M, D = x2d.shape
    N = w_main.shape[1]
    return pl.pallas_call(
        _inproj_kernel,
        grid=(M // tm,),
        in_specs=[pl.BlockSpec((tm, D), lambda i: (i, 0)),
                  pl.BlockSpec((D, N), lambda i: (0, 0)),
                  pl.BlockSpec((D, V7X_LANES), lambda i: (0, 0))],
        out_specs=[pl.BlockSpec((tm, N), lambda i: (i, 0)),
                   pl.BlockSpec((tm, V7X_LANES), lambda i: (i, 0))],
        out_shape=[jax.ShapeDtypeStruct((M, N), F32),
                   jax.ShapeDtypeStruct((M, V7X_LANES), F32)],
        compiler_params=_cparams(("parallel",)),
        name="inproj",
    )(x2d, w_main, w_gate)


def _mlstm_kernel(bif_ref, um_ref, v_ref, o_ref, gt_ref, cw_ref, cb_ref, wq_ref, wk_ref, mhg_ref,
                  y_ref, q_s, k_s, bcum_s, an_s, r_s, g_s, ml_s):
    h = pl.program_id(1)
    S, Dh = um_ref.shape
    NC, L = gt_ref.shape[1], gt_ref.shape[2]

    u = um_ref[...]
    row = _iota((S, Dh), 0)
    width = cw_ref.shape[0]
    acc = cb_ref[...] + cw_ref[width - 1:width, :] * u
    for j in range(width - 1):
        sh = width - 1 - j
        us = jnp.where(row >= sh, pltpu.roll(u, sh, axis=0), 0.0)
        acc = acc + cw_ref[j:j + 1, :] * us
    cb = (acc * jax.nn.sigmoid(acc)).astype(BF16)
    q_s[...] = jnp.dot(cb, wq_ref[...], preferred_element_type=F32)
    k_s[...] = jnp.dot(cb, wk_ref[...], preferred_element_type=F32) * (Dh ** -0.5)

    ig = gt_ref[0] + bif_ref[0, h]
    fp = gt_ref[1] + bif_ref[1, h]
    lf = jnp.minimum(fp, 0.0) - jnp.log1p(jnp.exp(-jnp.abs(fp)))
    tri = jnp.where(_iota((L, L), 0) <= _iota((L, L), 1), 1.0, 0.0).astype(F32)
    bcum = jnp.dot(lf, tri, precision=HIGHEST, preferred_element_type=F32)
    g = bcum[:, L - 1:L]
    a = g - bcum + ig
    m_loc = jnp.max(a, axis=-1, keepdims=True)
    bcum_s[...] = bcum
    an_s[...] = a - m_loc
    r_s[...] = ig - bcum
    g_s[...] = jnp.broadcast_to(g, (NC, V7X_LANES))
    ml_s[...] = jnp.broadcast_to(m_loc, (NC, V7X_LANES))

    eye = _iota((L, L), 0) == _iota((L, L), 1)
    causal = _iota((L, L), 0) >= _iota((L, L), 1)

    def to_col(row_vec):
        return jnp.sum(jnp.where(eye, row_vec, 0.0), axis=1, keepdims=True)

    def chunk(c, carry):
        C, n, m = carry
        t0 = pl.multiple_of(c * L, L)
        b_col = to_col(bcum_s[pl.ds(c, 1), :])
        an_col = to_col(an_s[pl.ds(c, 1), :])
        r_row = r_s[pl.ds(c, 1), :]
        g_c = g_s[pl.ds(c, 1), :][:, 0:1]
        ml_c = ml_s[pl.ds(c, 1), :][:, 0:1]
        qc = q_s[pl.ds(t0, L), :]
        kc = k_s[pl.ds(t0, L), :]
        vb = v_ref[pl.ds(t0, L), :].astype(BF16)
        qb = qc.astype(BF16)

        log_d = jnp.where(causal, b_col + r_row, -jnp.inf)
        log_inter = b_col + m
        m_t = jnp.maximum(log_inter, jnp.max(log_d, axis=1, keepdims=True))
        s = lax.dot_general(qb, kc.astype(BF16), (((1,), (1,)), ((), ())),
                            preferred_element_type=F32) * jnp.exp(log_d - m_t)
        inter = jnp.exp(log_inter - m_t)
        num = (jnp.dot(s.astype(BF16), vb, preferred_element_type=F32)
               + inter * jnp.dot(qb, C.astype(BF16), preferred_element_type=F32))
        den = jnp.sum(s, axis=1, keepdims=True) + inter * jnp.sum(qc * n, axis=1, keepdims=True)
        den = jnp.maximum(jnp.abs(den), jnp.exp(-m_t))
        hh = jax.nn.sigmoid(o_ref[pl.ds(t0, L), :]) * (num / den)
        mu = jnp.mean(hh, axis=-1, keepdims=True)
        hc = hh - mu
        var = jnp.mean(hc * hc, axis=-1, keepdims=True)
        y_ref[pl.ds(t0, L), :] = hc * lax.rsqrt(var + EPS) * mhg_ref[...]

        wk = jnp.exp(an_col) * kc
        Cc = lax.dot_general(wk.astype(BF16), vb, (((0,), (0,)), ((), ())), preferred_element_type=F32)
        nc = jnp.sum(wk, axis=0, keepdims=True)
        m_new = jnp.maximum(g_c + m, ml_c)
        s_old = jnp.exp(g_c + m - m_new)
        s_new = jnp.exp(ml_c - m_new)
        return s_old * C + s_new * Cc, s_old * n + s_new * nc, m_new

    init = (jnp.zeros((Dh, Dh), F32), jnp.zeros((1, Dh), F32), jnp.zeros((1, 1), F32))
    lax.fori_loop(0, NC, chunk, init, unroll=2)


def _mlstm(z3, gates, b_if, conv_w, conv_b, wq, wk, mh_g, H, Dh):
    B, S, _ = z3.shape
    NC, L = gates.shape[3], gates.shape[4]
    blk = lambda off: pl.BlockSpec((None, S, Dh), lambda b, h: (b, 0, off + h))
    return pl.pallas_call(
        _mlstm_kernel,
        grid=(B, H),
        in_specs=[pl.BlockSpec(memory_space=pltpu.SMEM),
                  blk(0), blk(H), blk(2 * H),
                  pl.BlockSpec((None, None, 2, NC, L), lambda b, h: (b, h, 0, 0, 0)),
                  pl.BlockSpec((conv_w.shape[0], Dh), lambda b, h: (0, h)),
                  pl.BlockSpec((1, Dh), lambda b, h: (0, h)),
                  pl.BlockSpec((None, Dh, Dh), lambda b, h: (h, 0, 0)),
                  pl.BlockSpec((None, Dh, Dh), lambda b, h: (h, 0, 0)),
                  pl.BlockSpec((1, Dh), lambda b, h: (0, h))],
        out_specs=pl.BlockSpec((None, S, Dh), lambda b, h: (b, 0, h)),
        out_shape=jax.ShapeDtypeStruct((B, S, H * Dh), F32),
        scratch_shapes=[pltpu.VMEM((S, Dh), F32), pltpu.VMEM((S, Dh), F32),
                        pltpu.VMEM((NC, L), F32), pltpu.VMEM((NC, L), F32), pltpu.VMEM((NC, L), F32),
                        pltpu.VMEM((NC, V7X_LANES), F32), pltpu.VMEM((NC, V7X_LANES), F32)],
        compiler_params=_cparams(("parallel", "parallel")),
        name="mlstm",
    )(b_if, z3, z3, z3, gates, conv_w, conv_b, wq, wk, mh_g)


def _s5_operators(lam_re, lam_im, log_dt, b_re, b_im, c_re, c_im, Lc):
    G, P = lam_re.shape
    Cn = b_re.shape[-1]
    lr, li = lam_re.astype(F32), lam_im.astype(F32)
    dt = jnp.exp(log_dt.astype(F32))[:, None]
    er = jnp.exp(lr * dt)
    ar, ai = er * jnp.cos(li * dt), er * jnp.sin(li * dt)
    mag2 = lr * lr + li * li
    xr, xi = ar - 1.0, ai
    cr = (xr * lr + xi * li) / mag2
    ci = (xi * lr - xr * li) / mag2
    bbr = cr[..., None] * b_re - ci[..., None] * b_im
    bbi = cr[..., None] * b_im + ci[..., None] * b_re
    pr, pi = [jnp.ones_like(ar)], [jnp.zeros_like(ai)]
    for _ in range(Lc):
        pr.append(pr[-1] * ar - pi[-1] * ai)
        pi.append(pr[-2] * ai + pi[-1] * ar)
    pwr, pwi = jnp.stack(pr, -1), jnp.stack(pi, -1)

    rev_r = pwr[:, :, Lc - 1::-1] if Lc > 1 else pwr[:, :, :1]
    rev_i = pwi[:, :, Lc - 1::-1] if Lc > 1 else pwi[:, :, :1]
    wr = jnp.einsum('gpt,gpc->gtcp', rev_r, bbr) - jnp.einsum('gpt,gpc->gtcp', rev_i, bbi)
    wi = jnp.einsum('gpt,gpc->gtcp', rev_r, bbi) + jnp.einsum('gpt,gpc->gtcp', rev_i, bbr)
    w2 = jnp.concatenate([wr, wi, wi, wr], axis=-1).reshape(G, Lc * Cn, 4 * P)

    car = c_re[..., None] * pwr[:, None] - c_im[..., None] * pwi[:, None]
    cai = c_re[..., None] * pwi[:, None] + c_im[..., None] * pwr[:, None]
    v_re = car[..., 1:].transpose(0, 2, 3, 1)
    v_im = -cai[..., 1:].transpose(0, 2, 3, 1)
    vmat = jnp.concatenate([v_re, v_im], axis=1).reshape(G, 2 * P, Lc * Cn)

    kern = (jnp.einsum('gcpk,gpd->gkdc', car[..., :Lc], bbr, precision=HIGHEST)
            - jnp.einsum('gcpk,gpd->gkdc', cai[..., :Lc], bbi, precision=HIGHEST))
    lag = jnp.arange(Lc)[None, :] - jnp.arange(Lc)[:, None]
    tm = jnp.where((lag >= 0)[None, :, :, None, None], kern[:, jnp.clip(lag, 0, Lc - 1)], 0.0)
    tmat = tm.transpose(0, 1, 3, 2, 4).reshape(G, Lc * Cn, Lc * Cn)

    pL, qL = pwr[..., Lc], pwi[..., Lc]
    pq = jnp.stack([jnp.concatenate([pL, pL], -1), jnp.concatenate([-qL, qL], -1),
                    jnp.concatenate([qL, -qL], -1)], axis=1)
    pq = jnp.concatenate([pq, jnp.zeros((G, V7X_SUBLANES - 3, 2 * P), F32)], axis=1)
    return w2.astype(BF16), tmat.astype(BF16), vmat.astype(BF16), pq


def _s5_kernel(z_ref, w2_ref, t_ref, v_ref, pq_ref, y_ref, vt_s, z1_s, z2_s, xp_s, yt_s, st_s, *, Lc, Cn):
    B, TT, LW = z_ref.shape
    GL = LW // Cn
    NCH = TT // Lc
    R = B * NCH
    P2 = xp_s.shape[1]

    @pl.when(pl.program_id(1) == 0)
    def _():
        st_s[...] = jnp.zeros_like(st_s)

    for t in range(Lc):
        a = z_ref[:, pl.ds(t, NCH, stride=Lc), :].reshape(R, LW).T
        for g in range(GL):
            vt_s[g, Cn * t:Cn * (t + 1), :] = a[Cn * g:Cn * (g + 1), :]

    for g in range(GL):
        ub = vt_s[g].T.astype(BF16)
        z = jnp.dot(ub, w2_ref[g], preferred_element_type=F32)
        z1_s[...] = z[:, :P2]
        z2_s[...] = z[:, P2:]
        pv, qv, q2 = pq_ref[g, 0:1, :], pq_ref[g, 1:2, :], pq_ref[g, 2:3, :]
        x, xs = st_s[g, 0], st_s[g, 1]
        for c in range(NCH):
            rows = pl.ds(c, B, stride=NCH)
            xp_s[rows, :] = x
            x, xs = x * pv + xs * qv + z1_s[rows, :], xs * pv + x * q2 + z2_s[rows, :]
        st_s[g, 0] = x
        st_s[g, 1] = xs
        y = (jnp.dot(ub, t_ref[g], preferred_element_type=F32)
             + jnp.dot(xp_s[...].astype(BF16), v_ref[g], preferred_element_type=F32))
        yt = y.T
        for t in range(Lc):
            yt_s[t, Cn * g:Cn * (g + 1), :] = yt[Cn * t:Cn * (t + 1), :]

    for t in range(Lc):
        y_ref[:, pl.ds(t, NCH, stride=Lc), :] = yt_s[t].T.reshape(B, NCH, LW)


def _s5(z3, col0, w2, tmat, vmat, pq, Cn, Lc, tt=512):
    B, S, _ = z3.shape
    G, W, _ = w2.shape
    P2 = vmat.shape[1]
    GL = V7X_LANES // Cn
    R = B * (tt // Lc)
    wspec = lambda a: pl.BlockSpec((GL,) + a.shape[1:], lambda j, t: (j, 0, 0))
    return pl.pallas_call(
        functools.partial(_s5_kernel, Lc=Lc, Cn=Cn),
        grid=(G // GL, S // tt),
        in_specs=[pl.BlockSpec((B, tt, V7X_LANES), lambda j, t: (0, t, col0 // V7X_LANES + j)),
                  wspec(w2), wspec(tmat), wspec(vmat), wspec(pq)],
        out_specs=pl.BlockSpec((B, tt, V7X_LANES), lambda j, t: (0, t, j)),
        out_shape=jax.ShapeDtypeStruct((B, S, G * Cn), F32),
        scratch_shapes=[pltpu.VMEM((GL, W, R), F32), pltpu.VMEM((R, P2), F32), pltpu.VMEM((R, P2), F32),
                        pltpu.VMEM((R, P2), F32), pltpu.VMEM((Lc, V7X_LANES, R), F32),
                        pltpu.VMEM((GL, 2, B, P2), F32)],
        compiler_params=_cparams(("parallel", "arbitrary")),
        name="s5",
    )(z3, w2, tmat, vmat, pq)


def _layer_norm(v, g, b):
    mu = jnp.mean(v, axis=-1, keepdims=True)
    vc = v - mu
    var = jnp.mean(vc * vc, axis=-1, keepdims=True)
    return vc * lax.rsqrt(var + EPS) * g + b


def _mixout_kernel(ym_ref, ys_ref, us_ref, dsk_ref, x_ref, wglu_ref, bglu_ref, s5g_ref, wom_ref, wos_ref,
                   g1_ref, b1_ref, wrh_ref, wrl_ref, br_ref, x1_ref, ri_ref, rw_ref, *, alpha, n_grp, epg):
    ys = ys_ref[...] + dsk_ref[...] * us_ref[...]
    gy = 0.5 * ys * (1.0 + jnp.tanh(math.sqrt(2.0 / math.pi) * (ys + 0.044715 * (ys * ys * ys))))
    glu = gy * jax.nn.sigmoid(jnp.dot(gy.astype(BF16), wglu_ref[...], preferred_element_type=F32)
                              + bglu_ref[...])
    ysn = glu * lax.rsqrt(jnp.mean(glu * glu, axis=-1, keepdims=True) + EPS) * s5g_ref[...]
    mix = (jnp.dot(ym_ref[...].astype(BF16), wom_ref[...], preferred_element_type=F32)
           + jnp.dot(ysn.astype(BF16), wos_ref[...], preferred_element_type=F32))
    x1 = _layer_norm(alpha * x_ref[...] + mix, g1_ref[...], b1_ref[...])
    x1_ref[...] = x1

    xh = x1.astype(BF16)
    xl = (x1 - xh.astype(F32)).astype(BF16)
    logits = (jnp.dot(xh, wrh_ref[...], preferred_element_type=F32)
              + jnp.dot(xl, wrh_ref[...], preferred_element_type=F32)
              + jnp.dot(xh, wrl_ref[...], preferred_element_type=F32) + br_ref[...])
    tm = logits.shape[0]
    lane = _iota((tm, V7X_LANES), 1)
    big = jnp.int32(V7X_LANES)
    neg = -jnp.inf
    gl = jnp.where(lane < n_grp, logits, neg)
    gmax = jnp.max(gl, axis=-1, keepdims=True)
    grp = jnp.min(jnp.where(gl == gmax, lane, big), axis=-1, keepdims=True)
    p_grp = 1.0 / jnp.sum(jnp.exp(gl - gmax), axis=-1, keepdims=True)
    lo = n_grp + grp * epg
    el = jnp.where((lane >= lo) & (lane < lo + epg), logits, neg)
    v1 = jnp.max(el, axis=-1, keepdims=True)
    i1 = jnp.min(jnp.where(el == v1, lane, big), axis=-1, keepdims=True)
    el2 = jnp.where(lane == i1, neg, el)
    v2 = jnp.max(el2, axis=-1, keepdims=True)
    i2 = jnp.min(jnp.where(el2 == v2, lane, big), axis=-1, keepdims=True)
    e2 = jnp.exp(v2 - v1)
    w1 = p_grp / (1.0 + e2)
    w2 = p_grp * e2 / (1.0 + e2)
    ri_ref[...] = jnp.where(lane == 0, i1 - n_grp, jnp.where(lane == 1, i2 - n_grp, 0))
    rw_ref[...] = jnp.where(lane == 0, w1, jnp.where(lane == 1, w2, 0.0))


def _mixout(ym, ys, z_main, us_col, dsk, x2d, wglu, bglu, s5g, wom, wos, g1, b1, wrh, wrl, br,
            alpha, n_grp, epg, tm=512):
    M, D = x2d.shape
    Dm, Ds = ym.shape[1], ys.shape[1]
    tile = lambda w: pl.BlockSpec((tm, w), lambda i: (i, 0))
    full = lambda a: pl.BlockSpec(a.shape, lambda i: (0,) * a.ndim)
    return pl.pallas_call(
        functools.partial(_mixout_kernel, alpha=alpha, n_grp=n_grp, epg=epg),
        grid=(M // tm,),
        in_specs=[tile(Dm), tile(Ds), pl.BlockSpec((tm, Ds), lambda i: (i, us_col // Ds)), full(dsk), tile(D),
                  full(wglu), full(bglu), full(s5g), full(wom), full(wos),
                  full(g1), full(b1), full(wrh), full(wrl), full(br)],
        out_specs=[tile(D), tile(V7X_LANES), tile(V7X_LANES)],
        out_shape=[jax.ShapeDtypeStruct((M, D), F32),
                   jax.ShapeDtypeStruct((M, V7X_LANES), I32),
                   jax.ShapeDtypeStruct((M, V7X_LANES), F32)],
        compiler_params=_cparams(("parallel",)),
        name="mixout",
    )(ym, ys, z_main, dsk, x2d, wglu, bglu, s5g, wom, wos, g1, b1, wrh, wrl, br)


def _dispatch_kernel(ri_ref, dest_ref, blk_ref, run_s, pst_s, *, n_exp, rows, nbp):
    ph = pl.program_id(0)
    i = pl.program_id(1)
    tm = ri_ref.shape[0]
    lane = _iota((tm, V7X_LANES), 1)
    oh0 = lane == ri_ref[:, 0:1]
    oh1 = lane == ri_ref[:, 1:2]
    cnt = jnp.where(oh0 | oh1, 1.0, 0.0).astype(F32)
    tile_cnt = jnp.sum(cnt, axis=0, keepdims=True)

    @pl.when((ph == 0) & (i == 0))
    def _():
        run_s[...] = jnp.zeros_like(run_s)

    @pl.when(ph == 0)
    def _():
        run_s[...] = run_s[...] + tile_cnt
        dest_ref[...] = jnp.zeros_like(dest_ref)

    @pl.when((ph == 1) & (i == 0))
    def _():
        counts = run_s[...]
        nblk = jnp.floor((counts + (rows - 1)) / rows)
        tri = jnp.where(_iota((V7X_LANES, V7X_LANES), 0) < _iota((V7X_LANES, V7X_LANES), 1), 1.0, 0.0)
        bstart = jnp.dot(jnp.broadcast_to(nblk, (V7X_SUBLANES, V7X_LANES)), tri.astype(F32),
                         precision=HIGHEST, preferred_element_type=F32)[0:1, :]
        pst_s[...] = bstart * rows
        bend = bstart + nblk
        j = _iota((nbp, V7X_LANES), 0).astype(F32)
        ln = _iota((nbp, V7X_LANES), 1)
        done = jnp.sum(jnp.where((ln < n_exp) & (bend <= j), 1.0, 0.0), axis=1, keepdims=True)
        blk = jnp.minimum(done, n_exp - 1.0)
        used = jnp.sum(jnp.where(ln == n_exp - 1, bend, 0.0), axis=1, keepdims=True)
        blk_ref[...] = jnp.where(ln == 0, blk, jnp.where(ln == 1, used, 0.0)).astype(I32)
        run_s[...] = jnp.zeros_like(run_s)

    @pl.when(ph == 1)
    def _():
        lower = jnp.where(_iota((tm, tm), 0) > _iota((tm, tm), 1), 1.0, 0.0).astype(BF16)
        excl = jnp.dot(lower, cnt.astype(BF16), preferred_element_type=F32)
        base = excl + run_s[...] + pst_s[...]
        d0 = jnp.sum(jnp.where(oh0, base, 0.0), axis=1, keepdims=True)
        d1 = jnp.sum(jnp.where(oh1, base, 0.0), axis=1, keepdims=True)
        dest_ref[...] = jnp.where(lane == 0, d0, jnp.where(lane == 1, d1, 0.0)).astype(I32)
        run_s[...] = run_s[...] + tile_cnt


def _dispatch(route_i, n_exp, rows, nbp, tm=512):
    M = route_i.shape[0]
    return pl.pallas_call(
        functools.partial(_dispatch_kernel, n_exp=n_exp, rows=rows, nbp=nbp),
        grid=(2, M // tm),
        in_specs=[pl.BlockSpec((tm, V7X_LANES), lambda p, i: (i, 0))],
        out_specs=[pl.BlockSpec((tm, V7X_LANES), lambda p, i: (i * p, 0)),
                   pl.BlockSpec((nbp, V7X_LANES), lambda p, i: (0, 0))],
        out_shape=[jax.ShapeDtypeStruct((M, V7X_LANES), I32),
                   jax.ShapeDtypeStruct((nbp, V7X_LANES), I32)],
        scratch_shapes=[pltpu.VMEM((1, V7X_LANES), F32), pltpu.VMEM((1, V7X_LANES), F32)],
        compiler_params=_cparams(("arbitrary", "arbitrary")),
        name="dispatch",
    )(route_i)


def _row_copy(src, dst, sem, r_src, r_dst):
    return pltpu.make_async_copy(src.at[pl.ds(r_src, 1), :], dst.at[pl.ds(r_dst, 1), :], sem)


def _scatter_kernel(dest_ref, x_ref, xs_in, xs_out, sem):
    del xs_in
    tm = x_ref.shape[0]
    base = pl.program_id(0) * (tm * TOP_K)

    def issue(r, _):
        for k in range(TOP_K):
            _row_copy(x_ref, xs_out, sem, r, dest_ref[base + r * TOP_K + k]).start()
        return 0

    lax.fori_loop(0, tm, issue, 0, unroll=8)

    def drain(r, _):
        for k in range(TOP_K):
            _row_copy(x_ref, xs_out, sem, 0, 0).wait()
        return 0

    lax.fori_loop(0, tm, drain, 0, unroll=8)


def _scatter(dest, x1, xs_zero, tm=256):
    M, D = x1.shape
    return pl.pallas_call(
        _scatter_kernel,
        grid_spec=pltpu.PrefetchScalarGridSpec(
            num_scalar_prefetch=1,
            grid=(M // tm,),
            in_specs=[pl.BlockSpec((tm, D), lambda i, d: (i, 0)),
                      pl.BlockSpec(memory_space=pl.ANY)],
            out_specs=pl.BlockSpec(memory_space=pl.ANY),
            scratch_shapes=[pltpu.SemaphoreType.DMA(())]),
        out_shape=jax.ShapeDtypeStruct(xs_zero.shape, xs_zero.dtype),
        input_output_aliases={2: 0},
        compiler_params=_cparams(("arbitrary",)),
        name="scatter",
    )(dest, x1, xs_zero)


def _experts_kernel(blk_ref, used_ref, xs_ref, wg_ref, wu_ref, wd_ref, o_ref, wg_s, wu_s, wd_s):
    j = pl.program_id(0)
    active = j < used_ref[0]

    @pl.when(active & ((j == 0) | (blk_ref[j] != blk_ref[jnp.maximum(j - 1, 0)])))
    def _():
        wg_s[...] = wg_ref[...].astype(BF16)
        wu_s[...] = wu_ref[...].astype(BF16)
        wd_s[...] = wd_ref[...].astype(BF16)

    @pl.when(active)
    def _():
        xb = xs_ref[...].astype(BF16)
        g = jnp.dot(xb, wg_s[...], preferred_element_type=F32)
        u = jnp.dot(xb, wu_s[...], preferred_element_type=F32)
        hmid = (g * jax.nn.sigmoid(g) * u).astype(BF16)
        o_ref[...] = jnp.dot(hmid, wd_s[...], preferred_element_type=F32)

    @pl.when(j >= used_ref[0])
    def _():
        o_ref[...] = jnp.zeros_like(o_ref)


def _experts(blk_e, used, xs, wg, wu, wd, layer, rows):
    P, D = xs.shape
    De = wg.shape[3]
    nb = P // rows
    wspec = lambda a, b_: pl.BlockSpec((None, None, a, b_), lambda j, b, u: (layer, b[j], 0, 0))
    return pl.pallas_call(
        _experts_kernel,
        grid_spec=pltpu.PrefetchScalarGridSpec(
            num_scalar_prefetch=2,
            grid=(nb,),
            in_specs=[pl.BlockSpec((rows, D), lambda j, b, u: (j, 0)),
                      wspec(D, De), wspec(D, De), wspec(De, D)],
            out_specs=pl.BlockSpec((rows, D), lambda j, b, u: (j, 0)),
            scratch_shapes=[pltpu.VMEM((D, De), BF16), pltpu.VMEM((D, De), BF16), pltpu.VMEM((De, D), BF16)]),
        out_shape=jax.ShapeDtypeStruct((P, D), F32),
        compiler_params=_cparams(("arbitrary",)),
        name="experts",
    )(blk_e, used, xs, wg, wu, wd)


def _post_kernel(dest_ref, x1_ref, rw_ref, p_ref, os_hbm, g2_ref, b2_ref, wpg_ref, bpg_ref, wpp_ref, pg_ref,
                 out_ref, gbuf, sem, *, alpha):
    i = pl.program_id(0)
    tm = x1_ref.shape[0]
    slot = lax.rem(i, 2)

    def start_gather(tile, slot_):
        base = tile * (tm * TOP_K)

        def issue(r, _):
            for k in range(TOP_K):
                _row_copy(os_hbm, gbuf.at[slot_, k], sem.at[slot_], dest_ref[base + r * TOP_K + k], r).start()
            return 0

        lax.fori_loop(0, tm, issue, 0, unroll=8)

    @pl.when(i == 0)
    def _():
        start_gather(0, 0)

    @pl.when(i + 1 < pl.num_programs(0))
    def _():
        start_gather(i + 1, 1 - slot)

    def drain(r, _):
        for k in range(TOP_K):
            _row_copy(os_hbm, gbuf.at[slot, k], sem.at[slot], 0, 0).wait()
        return 0

    lax.fori_loop(0, tm, drain, 0, unroll=8)

    rw = rw_ref[...]
    ffn = rw[:, 0:1] * gbuf[slot, 0] + rw[:, 1:2] * gbuf[slot, 1]
    x2 = _layer_norm(alpha * x1_ref[...] + ffn, g2_ref[...], b2_ref[...])
    gate = jax.nn.sigmoid(jnp.dot(x2.astype(BF16), wpg_ref[...], preferred_element_type=F32) + bpg_ref[...])
    pp = jnp.dot(p_ref[...].astype(BF16), wpp_ref[...], preferred_element_type=F32)
    ple = pp * lax.rsqrt(jnp.mean(pp * pp, axis=-1, keepdims=True) + EPS) * pg_ref[...]
    out_ref[...] = x2 + gate * ple


def _post(dest, x1, route_w, p3d, layer, os, g2, b2, wpg, bpg, wpp, pg, alpha, tm=256):
    M, D = x1.shape
    Dp = p3d.shape[2]
    tile = lambda w: pl.BlockSpec((tm, w), lambda i, d: (i, 0))
    full = lambda a: pl.BlockSpec(a.shape, lambda i, d: (0,) * a.ndim)
    return pl.pallas_call(
        functools.partial(_post_kernel, alpha=alpha),
        grid_spec=pltpu.PrefetchScalarGridSpec(
            num_scalar_prefetch=1,
            grid=(M // tm,),
            in_specs=[tile(D), tile(V7X_LANES), pl.BlockSpec((None, tm, Dp), lambda i, d: (layer, i, 0)),
                      pl.BlockSpec(memory_space=pl.ANY),
                      full(g2), full(b2), full(wpg), full(bpg), full(wpp), full(pg)],
            out_specs=tile(D),
            scratch_shapes=[pltpu.VMEM((2, TOP_K, tm, D), F32), pltpu.SemaphoreType.DMA((2,))]),
        out_shape=jax.ShapeDtypeStruct((M, D), F32),
        compiler_params=_cparams(("arbitrary",)),
        name="post",
    )(dest, x1, route_w, p3d, os, g2, b2, wpg, bpg, wpp, pg)


def kernel(x, p, w_in, conv_w, conv_b, w_q, w_k, b_i, b_f, mh_g, lam_re, lam_im, log_dt, b_re, b_im, c_re, c_im, d_skip, w_glu, b_glu, s5_g, w_out, ln1_g, ln1_b, w_grp, b_grp, w_rt, b_rt, w_eg, w_eu, w_ed, ln2_g, ln2_b, w_pg, b_pg, w_pp, ple_g):
    B, S, D = x.shape
    depth = w_in.shape[0]
    H, Dh = w_q.shape[1], w_q.shape[2]
    Dm = H * Dh
    G, Cn = d_skip.shape[1], d_skip.shape[2]
    Ds = G * Cn
    n_grp = w_grp.shape[-1]
    n_exp = w_eg.shape[1]
    epg = n_exp // n_grp
    M = B * S
    A = M * TOP_K
    alpha = (2 * depth) ** 0.25
    NC = S // MLSTM_CHUNK
    nblocks = A // MOE_ROWS + n_exp
    nbp = -(-nblocks // V7X_SUBLANES) * V7X_SUBLANES
    row2 = lambda a: a.reshape(1, -1).astype(F32)

    xc = x.reshape(M, D).astype(F32)
    for l in range(depth):
        wi = w_in[l]
        w_main = jnp.concatenate([wi[:, :3 * Dm], wi[:, 3 * Dm + 2 * H:]], axis=1).astype(BF16)
        w_gate = jnp.pad(wi[:, 3 * Dm:3 * Dm + 2 * H], ((0, 0), (0, V7X_LANES - 2 * H))).astype(BF16)
        w_router = jnp.pad(jnp.concatenate([w_grp[l], w_rt[l]], axis=1).astype(F32),
                           ((0, 0), (0, V7X_LANES - n_grp - n_exp)))
        w_router_hi = w_router.astype(BF16)
        w_router_lo = (w_router - w_router_hi.astype(F32)).astype(BF16)
        b_router = jnp.pad(jnp.concatenate([b_grp[l], b_rt[l]]).astype(F32),
                           (0, V7X_LANES - n_grp - n_exp)).reshape(1, -1)
        s5_ops = _s5_operators(lam_re[l], lam_im[l], log_dt[l], b_re[l], b_im[l], c_re[l], c_im[l], S5_CHUNK)

        z_main, z_gate = _inproj(xc, w_main, w_gate)

        gates = z_gate[:, :2 * H].reshape(B, NC, MLSTM_CHUNK, 2, H).transpose(0, 4, 3, 1, 2)
        y_m = _mlstm(z_main.reshape(B, S, -1), gates, jnp.stack([b_i[l], b_f[l]]).astype(F32),
                     conv_w[l].astype(F32), row2(conv_b[l]), w_q[l].astype(BF16), w_k[l].astype(BF16),
                     row2(mh_g[l]), H, Dh).reshape(M, Dm)

        y_s = _s5(z_main.reshape(B, S, -1), 3 * Dm, *s5_ops, Cn, S5_CHUNK).reshape(M, Ds)

        wo = w_out[l].astype(BF16)
        x1, route_i, route_w = _mixout(y_m, y_s, z_main, 3 * Dm, row2(d_skip[l]), xc,
                                       w_glu[l].astype(BF16), row2(b_glu[l]), row2(s5_g[l]),
                                       wo[:Dm], wo[Dm:], row2(ln1_g[l]), row2(ln1_b[l]),
                                       w_router_hi, w_router_lo, b_router, alpha, n_grp, epg)

        dest_l, blk_l = _dispatch(route_i, n_exp, MOE_ROWS, nbp)
        dest = dest_l[:, :TOP_K].reshape(A)
        xs = _scatter(dest, x1, jnp.zeros((nblocks * MOE_ROWS, D), F32))
        os = _experts(blk_l[:nblocks, 0], blk_l[:1, 1], xs,
                      w_eg, w_eu, w_ed, l, MOE_ROWS)
        xc = _post(dest, x1, route_w, p.reshape(depth, M, -1), l, os, row2(ln2_g[l]), row2(ln2_b[l]),
                   w_pg[l].astype(BF16), row2(b_pg[l]), w_pp[l].astype(BF16), row2(ple_g[l]), alpha)
    return xc.reshape(B, S, D).astype(x.dtype)
```

```python
import functools
import math

import jax
import jax.numpy as jnp
from jax import lax
from jax.experimental import pallas as pl
from jax.experimental.pallas import tpu as pltpu

F32 = jnp.float32
BF16 = jnp.bfloat16
I32 = jnp.int32
HIGHEST = lax.Precision.HIGHEST

EPS = 1e-5
V7X_LANES = 128
V7X_SUBLANES = 8
VMEM_LIMIT = 56 * 1024 * 1024

MLSTM_CHUNK = 256
S5_CHUNK = 16
MOE_ROWS = 256
TOP_K = 2


def _cparams(sem):
    return pltpu.CompilerParams(dimension_semantics=sem, vmem_limit_bytes=VMEM_LIMIT)


def _iota(shape, axis):
    return lax.broadcasted_iota(I32, shape, axis)


U32 = jnp.uint32
_HI16 = 0xFFFF0000


def _pack_bf16_pairs(v):
    W = v.shape[1] // 2
    bits = lax.bitcast_convert_type(v.astype(BF16).astype(F32), U32)
    return (bits[:, :W] >> 16) | (bits[:, W:] & U32(_HI16))


def _unpack_bf16_pairs(w):
    return lax.bitcast_convert_type(w << 16, F32), lax.bitcast_convert_type(w & U32(_HI16), F32)


def _inproj_kernel(x_ref, wm_ref, wg_ref, zm_ref, zg_ref):
    xb = x_ref[...].astype(BF16)
    zm_ref[...] = jnp.dot(xb, wm_ref[...], preferred_element_type=F32)
    zg_ref[...] = jnp.dot(xb, wg_ref[...], preferred_element_type=F32)


def _inproj(x2d, w_main, w_gate, tm=512):
    M, D = x2d.shape
    N = w_main.shape[1]
    return pl.pallas_call(
        _inproj_kernel,
        grid=(M // tm,),
        in_specs=[pl.BlockSpec((tm, D), lambda i: (i, 0)),
                  pl.BlockSpec((D, N), lambda i: (0, 0)),
                  pl.BlockSpec((D, V7X_LANES), lambda i: (0, 0))],
        out_specs=[pl.BlockSpec((tm, N), lambda i: (i, 0)),
                   pl.BlockSpec((tm, V7X_LANES), lambda i: (i, 0))],
        out_shape=[jax.ShapeDtypeStruct((M, N), F32),
                   jax.ShapeDtypeStruct((M, V7X_LANES), F32)],
        compiler_params=_cparams(("parallel",)),
        name="inproj",
    )(x2d, w_main, w_gate)


def _mlstm_kernel(bif_ref, um_ref, v_ref, o_ref, gt_ref, cw_ref, cb_ref, wq_ref, wk_ref, mhg_ref,
                  y_ref, q_s, k_s, bcum_s, an_s, r_s, g_s, ml_s):
    h = pl.program_id(1)
    S, Dh = um_ref.shape
    NC, L = gt_ref.shape[1], gt_ref.shape[2]

    u = um_ref[...]
    row = _iota((S, Dh), 0)
    width = cw_ref.shape[0]
    acc = cb_ref[...] + cw_ref[width - 1:width, :] * u
    for j in range(width - 1):
        sh = width - 1 - j
        us = jnp.where(row >= sh, pltpu.roll(u, sh, axis=0), 0.0)
        acc = acc + cw_ref[j:j + 1, :] * us
    cb = (acc * jax.nn.sigmoid(acc)).astype(BF16)
    q_s[...] = jnp.dot(cb, wq_ref[...], preferred_element_type=F32)
    k_s[...] = jnp.dot(cb, wk_ref[...], preferred_element_type=F32) * (Dh ** -0.5)

    ig = gt_ref[0] + bif_ref[0, h]
    fp = gt_ref[1] + bif_ref[1, h]
    lf = jnp.minimum(fp, 0.0) - jnp.log1p(jnp.exp(-jnp.abs(fp)))
    tri = jnp.where(_iota((L, L), 0) <= _iota((L, L), 1), 1.0, 0.0).astype(F32)
    bcum = jnp.dot(lf, tri, precision=HIGHEST, preferred_element_type=F32)
    g = bcum[:, L - 1:L]
    a = g - bcum + ig
    m_loc = jnp.max(a, axis=-1, keepdims=True)
    bcum_s[...] = bcum
    an_s[...] = a - m_loc
    r_s[...] = ig - bcum
    g_s[...] = jnp.broadcast_to(g, (NC, V7X_LANES))
    ml_s[...] = jnp.broadcast_to(m_loc, (NC, V7X_LANES))

    eye = _iota((L, L), 0) == _iota((L, L), 1)
    causal = _iota((L, L), 0) >= _iota((L, L), 1)

    def to_col(row_vec):
        return jnp.sum(jnp.where(eye, row_vec, 0.0), axis=1, keepdims=True)

    def chunk(c, carry):
        C, n, m = carry
        t0 = pl.multiple_of(c * L, L)
        b_col = to_col(bcum_s[pl.ds(c, 1), :])
        an_col = to_col(an_s[pl.ds(c, 1), :])
        r_row = r_s[pl.ds(c, 1), :]
        g_c = g_s[pl.ds(c, 1), :][:, 0:1]
        ml_c = ml_s[pl.ds(c, 1), :][:, 0:1]
        qc = q_s[pl.ds(t0, L), :]
        kc = k_s[pl.ds(t0, L), :]
        vb = v_ref[pl.ds(t0, L), :].astype(BF16)
        qb = qc.astype(BF16)

        log_d = jnp.where(causal, b_col + r_row, -jnp.inf)
        log_inter = b_col + m
        m_t = jnp.maximum(log_inter, jnp.max(log_d, axis=1, keepdims=True))
        s = lax.dot_general(qb, kc.astype(BF16), (((1,), (1,)), ((), ())),
                            preferred_element_type=F32) * jnp.exp(log_d - m_t)
        inter = jnp.exp(log_inter - m_t)
        num = (jnp.dot(s.astype(BF16), vb, preferred_element_type=F32)
               + inter * jnp.dot(qb, C.astype(BF16), preferred_element_type=F32))
        den = jnp.sum(s, axis=1, keepdims=True) + inter * jnp.sum(qc * n, axis=1, keepdims=True)
        den = jnp.maximum(jnp.abs(den), jnp.exp(-m_t))
        hh = jax.nn.sigmoid(o_ref[pl.ds(t0, L), :]) * (num / den)
        mu = jnp.mean(hh, axis=-1, keepdims=True)
        hc = hh - mu
        var = jnp.mean(hc * hc, axis=-1, keepdims=True)
        y_ref[pl.ds(t0, L), :] = hc * lax.rsqrt(var + EPS) * mhg_ref[...]

        wk = jnp.exp(an_col) * kc
        Cc = lax.dot_general(wk.astype(BF16), vb, (((0,), (0,)), ((), ())), preferred_element_type=F32)
        nc = jnp.sum(wk, axis=0, keepdims=True)
        m_new = jnp.maximum(g_c + m, ml_c)
        s_old = jnp.exp(g_c + m - m_new)
        s_new = jnp.exp(ml_c - m_new)
        return s_old * C + s_new * Cc, s_old * n + s_new * nc, m_new

    init = (jnp.zeros((Dh, Dh), F32), jnp.zeros((1, Dh), F32), jnp.zeros((1, 1), F32))
    lax.fori_loop(0, NC, chunk, init, unroll=2)


def _mlstm(z3, gates, b_if, conv_w, conv_b, wq, wk, mh_g, H, Dh):
    B, S, _ = z3.shape
    NC, L = gates.shape[3], gates.shape[4]
    blk = lambda off: pl.BlockSpec((None, S, Dh), lambda b, h: (b, 0, off + h))
    return pl.pallas_call(
        _mlstm_kernel,
        grid=(B, H),
        in_specs=[pl.BlockSpec(memory_space=pltpu.SMEM),
                  blk(0), blk(H), blk(2 * H),
                  pl.BlockSpec((None, None, 2, NC, L), lambda b, h: (b, h, 0, 0, 0)),
                  pl.BlockSpec((conv_w.shape[0], Dh), lambda b, h: (0, h)),
                  pl.BlockSpec((1, Dh), lambda b, h: (0, h)),
                  pl.BlockSpec((None, Dh, Dh), lambda b, h: (h, 0, 0)),
                  pl.BlockSpec((None, Dh, Dh), lambda b, h: (h, 0, 0)),
                  pl.BlockSpec((1, Dh), lambda b, h: (0, h))],
        out_specs=pl.BlockSpec((None, S, Dh), lambda b, h: (b, 0, h)),
        out_shape=jax.ShapeDtypeStruct((B, S, H * Dh), F32),
        scratch_shapes=[pltpu.VMEM((S, Dh), F32), pltpu.VMEM((S, Dh), F32),
                        pltpu.VMEM((NC, L), F32), pltpu.VMEM((NC, L), F32), pltpu.VMEM((NC, L), F32),
                        pltpu.VMEM((NC, V7X_LANES), F32), pltpu.VMEM((NC, V7X_LANES), F32)],
        compiler_params=_cparams(("parallel", "parallel")),
        name="mlstm",
    )(b_if, z3, z3, z3, gates, conv_w, conv_b, wq, wk, mh_g)


def _s5_operators(lam_re, lam_im, log_dt, b_re, b_im, c_re, c_im, Lc):
    G, P = lam_re.shape
    Cn = b_re.shape[-1]
    lr, li = lam_re.astype(F32), lam_im.astype(F32)
    dt = jnp.exp(log_dt.astype(F32))[:, None]
    er = jnp.exp(lr * dt)
    ar, ai = er * jnp.cos(li * dt), er * jnp.sin(li * dt)
    mag2 = lr * lr + li * li
    xr, xi = ar - 1.0, ai
    cr = (xr * lr + xi * li) / mag2
    ci = (xi * lr - xr * li) / mag2
    bbr = cr[..., None] * b_re - ci[..., None] * b_im
    bbi = cr[..., None] * b_im + ci[..., None] * b_re
    pr, pi = [jnp.ones_like(ar)], [jnp.zeros_like(ai)]
    for _ in range(Lc):
        pr.append(pr[-1] * ar - pi[-1] * ai)
        pi.append(pr[-2] * ai + pi[-1] * ar)
    pwr, pwi = jnp.stack(pr, -1), jnp.stack(pi, -1)

    rev_r = pwr[:, :, Lc - 1::-1] if Lc > 1 else pwr[:, :, :1]
    rev_i = pwi[:, :, Lc - 1::-1] if Lc > 1 else pwi[:, :, :1]
    wr = jnp.einsum('gpt,gpc->gtcp', rev_r, bbr) - jnp.einsum('gpt,gpc->gtcp', rev_i, bbi)
    wi = jnp.einsum('gpt,gpc->gtcp', rev_r, bbi) + jnp.einsum('gpt,gpc->gtcp', rev_i, bbr)
    w2 = jnp.concatenate([wr, wi, wi, wr], axis=-1).reshape(G, Lc * Cn, 4 * P)

    car = c_re[..., None] * pwr[:, None] - c_im[..., None] * pwi[:, None]
    cai = c_re[..., None] * pwi[:, None] + c_im[..., None] * pwr[:, None]
    v_re = car[..., 1:].transpose(0, 2, 3, 1)
    v_im = -cai[..., 1:].transpose(0, 2, 3, 1)
    vmat = jnp.concatenate([v_re, v_im], axis=1).reshape(G, 2 * P, Lc * Cn)

    kern = (jnp.einsum('gcpk,gpd->gkdc', car[..., :Lc], bbr, precision=HIGHEST)
            - jnp.einsum('gcpk,gpd->gkdc', cai[..., :Lc], bbi, precision=HIGHEST))
    lag = jnp.arange(Lc)[None, :] - jnp.arange(Lc)[:, None]
    tm = jnp.where((lag >= 0)[None, :, :, None, None], kern[:, jnp.clip(lag, 0, Lc - 1)], 0.0)
    tmat = tm.transpose(0, 1, 3, 2, 4).reshape(G, Lc * Cn, Lc * Cn)

    pL, qL = pwr[..., Lc], pwi[..., Lc]
    pq = jnp.stack([jnp.concatenate([pL, pL], -1), jnp.concatenate([-qL, qL], -1),
                    jnp.concatenate([qL, -qL], -1)], axis=1)
    pq = jnp.concatenate([pq, jnp.zeros((G, V7X_SUBLANES - 3, 2 * P), F32)], axis=1)
    return w2.astype(BF16), tmat.astype(BF16), vmat.astype(BF16), pq


def _s5_kernel(z_ref, w2_ref, t_ref, v_ref, pq_ref, y_ref, vt_s, z1_s, z2_s, xp_s, yt_s, st_s, *, Lc, Cn):
    B, TT, LW = z_ref.shape
    GL = LW // Cn
    NCH = TT // Lc
    R = B * NCH
    P2 = xp_s.shape[1]

    @pl.when(pl.program_id(1) == 0)
    def _():
        st_s[...] = jnp.zeros_like(st_s)

    for t in range(Lc):
        a = z_ref[:, pl.ds(t, NCH, stride=Lc), :].reshape(R, LW).T
        for g in range(GL):
            vt_s[g, Cn * t:Cn * (t + 1), :] = a[Cn * g:Cn * (g + 1), :]

    for g in range(GL):
        ub = vt_s[g].T.astype(BF16)
        z = jnp.dot(ub, w2_ref[g], preferred_element_type=F32)
        z1_s[...] = z[:, :P2]
        z2_s[...] = z[:, P2:]
        pv, qv, q2 = pq_ref[g, 0:1, :], pq_ref[g, 1:2, :], pq_ref[g, 2:3, :]
        x, xs = st_s[g, 0], st_s[g, 1]
        for c in range(NCH):
            rows = pl.ds(c, B, stride=NCH)
            xp_s[rows, :] = x
            x, xs = x * pv + xs * qv + z1_s[rows, :], xs * pv + x * q2 + z2_s[rows, :]
        st_s[g, 0] = x
        st_s[g, 1] = xs
        y = (jnp.dot(ub, t_ref[g], preferred_element_type=F32)
             + jnp.dot(xp_s[...].astype(BF16), v_ref[g], preferred_element_type=F32))
        yt = y.T
        for t in range(Lc):
            yt_s[t, Cn * g:Cn * (g + 1), :] = yt[Cn * t:Cn * (t + 1), :]

    for t in range(Lc):
        y_ref[:, pl.ds(t, NCH, stride=Lc), :] = yt_s[t].T.reshape(B, NCH, LW)


def _s5(z3, col0, w2, tmat, vmat, pq, layer, Cn, Lc, tt=512):
    B, S, _ = z3.shape
    _, G, W, _ = w2.shape
    P2 = vmat.shape[2]
    GL = V7X_LANES // Cn
    R = B * (tt // Lc)
    wspec = lambda a: pl.BlockSpec((None, GL) + a.shape[2:], lambda j, t: (layer, j, 0, 0))
    return pl.pallas_call(
        functools.partial(_s5_kernel, Lc=Lc, Cn=Cn),
        grid=(G // GL, S // tt),
        in_specs=[pl.BlockSpec((B, tt, V7X_LANES), lambda j, t: (0, t, col0 // V7X_LANES + j)),
                  wspec(w2), wspec(tmat), wspec(vmat), wspec(pq)],
        out_specs=pl.BlockSpec((B, tt, V7X_LANES), lambda j, t: (0, t, j)),
        out_shape=jax.ShapeDtypeStruct((B, S, G * Cn), F32),
        scratch_shapes=[pltpu.VMEM((GL, W, R), F32), pltpu.VMEM((R, P2), F32), pltpu.VMEM((R, P2), F32),
                        pltpu.VMEM((R, P2), F32), pltpu.VMEM((Lc, V7X_LANES, R), F32),
                        pltpu.VMEM((GL, 2, B, P2), F32)],
        compiler_params=_cparams(("parallel", "arbitrary")),
        name="s5",
    )(z3, w2, tmat, vmat, pq)


def _layer_norm(v, g, b):
    mu = jnp.mean(v, axis=-1, keepdims=True)
    vc = v - mu
    var = jnp.mean(vc * vc, axis=-1, keepdims=True)
    return vc * lax.rsqrt(var + EPS) * g + b


def _mixout_kernel(ym_ref, ys_ref, us_ref, dsk_ref, x_ref, wglu_ref, bglu_ref, s5g_ref, wom_ref, wos_ref,
                   g1_ref, b1_ref, wrh_ref, wrl_ref, br_ref, x1_ref, x1p_ref, ri_ref, rw_ref,
                   *, alpha, n_grp, epg):
    ys = ys_ref[...] + dsk_ref[...] * us_ref[...]
    gy = 0.5 * ys * (1.0 + jnp.tanh(math.sqrt(2.0 / math.pi) * (ys + 0.044715 * (ys * ys * ys))))
    glu = gy * jax.nn.sigmoid(jnp.dot(gy.astype(BF16), wglu_ref[...], preferred_element_type=F32)
                              + bglu_ref[...])
    ysn = glu * lax.rsqrt(jnp.mean(glu * glu, axis=-1, keepdims=True) + EPS) * s5g_ref[...]
    mix = (jnp.dot(ym_ref[...].astype(BF16), wom_ref[...], preferred_element_type=F32)
           + jnp.dot(ysn.astype(BF16), wos_ref[...], preferred_element_type=F32))
    x1 = _layer_norm(alpha * x_ref[...] + mix, g1_ref[...], b1_ref[...])
    x1_ref[...] = x1
    x1p_ref[...] = _pack_bf16_pairs(x1)

    xh = x1.astype(BF16)
    xl = (x1 - xh.astype(F32)).astype(BF16)
    logits = (jnp.dot(xh, wrh_ref[...], preferred_element_type=F32)
              + jnp.dot(xl, wrh_ref[...], preferred_element_type=F32)
              + jnp.dot(xh, wrl_ref[...], preferred_element_type=F32) + br_ref[...])
    tm = logits.shape[0]
    lane = _iota((tm, V7X_LANES), 1)
    big = jnp.int32(V7X_LANES)
    neg = -jnp.inf
    gl = jnp.where(lane < n_grp, logits, neg)
    gmax = jnp.max(gl, axis=-1, keepdims=True)
    grp = jnp.min(jnp.where(gl == gmax, lane, big), axis=-1, keepdims=True)
    p_grp = 1.0 / jnp.sum(jnp.exp(gl - gmax), axis=-1, keepdims=True)
    lo = n_grp + grp * epg
    el = jnp.where((lane >= lo) & (lane < lo + epg), logits, neg)
    v1 = jnp.max(el, axis=-1, keepdims=True)
    i1 = jnp.min(jnp.where(el == v1, lane, big), axis=-1, keepdims=True)
    el2 = jnp.where(lane == i1, neg, el)
    v2 = jnp.max(el2, axis=-1, keepdims=True)
    i2 = jnp.min(jnp.where(el2 == v2, lane, big), axis=-1, keepdims=True)
    e2 = jnp.exp(v2 - v1)
    w1 = p_grp / (1.0 + e2)
    w2 = p_grp * e2 / (1.0 + e2)
    ri_ref[...] = jnp.where(lane == 0, i1 - n_grp, jnp.where(lane == 1, i2 - n_grp, 0))
    rw_ref[...] = jnp.where(lane == 0, w1, jnp.where(lane == 1, w2, 0.0))


def _mixout(ym, ys, z_main, us_col, dsk, x2d, wglu, bglu, s5g, wom, wos, g1, b1, wrh, wrl, br,
            alpha, n_grp, epg, tm=512):
    M, D = x2d.shape
    Dm, Ds = ym.shape[1], ys.shape[1]
    tile = lambda w: pl.BlockSpec((tm, w), lambda i: (i, 0))
    full = lambda a: pl.BlockSpec(a.shape, lambda i: (0,) * a.ndim)
    return pl.pallas_call(
        functools.partial(_mixout_kernel, alpha=alpha, n_grp=n_grp, epg=epg),
        grid=(M // tm,),
        in_specs=[tile(Dm), tile(Ds), pl.BlockSpec((tm, Ds), lambda i: (i, us_col // Ds)), full(dsk), tile(D),
                  full(wglu), full(bglu), full(s5g), full(wom), full(wos),
                  full(g1), full(b1), full(wrh), full(wrl), full(br)],
        out_specs=[tile(D), tile(D // 2), tile(V7X_LANES), tile(V7X_LANES)],
        out_shape=[jax.ShapeDtypeStruct((M, D), F32),
                   jax.ShapeDtypeStruct((M, D // 2), U32),
                   jax.ShapeDtypeStruct((M, V7X_LANES), I32),
                   jax.ShapeDtypeStruct((M, V7X_LANES), F32)],
        compiler_params=_cparams(("parallel",)),
        name="mixout",
    )(ym, ys, z_main, dsk, x2d, wglu, bglu, s5g, wom, wos, g1, b1, wrh, wrl, br)


def _dispatch_kernel(ri_ref, dest_ref, blk_ref, run_s, pst_s, *, n_exp, rows, nbp):
    ph = pl.program_id(0)
    i = pl.program_id(1)
    tm = ri_ref.shape[0]
    lane = _iota((tm, V7X_LANES), 1)
    oh0 = lane == ri_ref[:, 0:1]
    oh1 = lane == ri_ref[:, 1:2]
    cnt = jnp.where(oh0 | oh1, 1.0, 0.0).astype(F32)
    tile_cnt = jnp.sum(cnt, axis=0, keepdims=True)

    @pl.when((ph == 0) & (i == 0))
    def _():
        run_s[...] = jnp.zeros_like(run_s)

    @pl.when(ph == 0)
    def _():
        run_s[...] = run_s[...] + tile_cnt
        dest_ref[...] = jnp.zeros_like(dest_ref)

    @pl.when((ph == 1) & (i == 0))
    def _():
        counts = run_s[...]
        nblk = jnp.floor((counts + (rows - 1)) / rows)
        tri = jnp.where(_iota((V7X_LANES, V7X_LANES), 0) < _iota((V7X_LANES, V7X_LANES), 1), 1.0, 0.0)
        bstart = jnp.dot(jnp.broadcast_to(nblk, (V7X_SUBLANES, V7X_LANES)), tri.astype(F32),
                         precision=HIGHEST, preferred_element_type=F32)[0:1, :]
        pst_s[...] = bstart * rows
        bend = bstart + nblk
        j = _iota((nbp, V7X_LANES), 0).astype(F32)
        ln = _iota((nbp, V7X_LANES), 1)
        done = jnp.sum(jnp.where((ln < n_exp) & (bend <= j), 1.0, 0.0), axis=1, keepdims=True)
        blk = jnp.minimum(done, n_exp - 1.0)
        used = jnp.sum(jnp.where(ln == n_exp - 1, bend, 0.0), axis=1, keepdims=True)
        mine = ln.astype(F32) == blk
        cnt_j = jnp.sum(jnp.where(mine, counts, 0.0), axis=1, keepdims=True)
        bst_j = jnp.sum(jnp.where(mine, bstart, 0.0), axis=1, keepdims=True)
        valid = jnp.clip(cnt_j - (j[:, 0:1] - bst_j) * rows, 0.0, float(rows))
        blk_ref[...] = jnp.where(ln == 0, blk, jnp.where(ln == 1, used, jnp.where(ln == 2, valid, 0.0))
                                 ).astype(I32)
        run_s[...] = jnp.zeros_like(run_s)

    @pl.when(ph == 1)
    def _():
        lower = jnp.where(_iota((tm, tm), 0) > _iota((tm, tm), 1), 1.0, 0.0).astype(BF16)
        excl = jnp.dot(lower, cnt.astype(BF16), preferred_element_type=F32)
        base = excl + run_s[...] + pst_s[...]
        d0 = jnp.sum(jnp.where(oh0, base, 0.0), axis=1, keepdims=True)
        d1 = jnp.sum(jnp.where(oh1, base, 0.0), axis=1, keepdims=True)
        dest_ref[...] = jnp.where(lane == 0, d0, jnp.where(lane == 1, d1, 0.0)).astype(I32)
        run_s[...] = run_s[...] + tile_cnt


def _dispatch(route_i, n_exp, rows, nbp, tm=512):
    M = route_i.shape[0]
    return pl.pallas_call(
        functools.partial(_dispatch_kernel, n_exp=n_exp, rows=rows, nbp=nbp),
        grid=(2, M // tm),
        in_specs=[pl.BlockSpec((tm, V7X_LANES), lambda p, i: (i, 0))],
        out_specs=[pl.BlockSpec((tm, V7X_LANES), lambda p, i: (i * p, 0)),
                   pl.BlockSpec((nbp, V7X_LANES), lambda p, i: (0, 0))],
        out_shape=[jax.ShapeDtypeStruct((M, V7X_LANES), I32),
                   jax.ShapeDtypeStruct((nbp, V7X_LANES), I32)],
        scratch_shapes=[pltpu.VMEM((1, V7X_LANES), F32), pltpu.VMEM((1, V7X_LANES), F32)],
        compiler_params=_cparams(("arbitrary", "arbitrary")),
        name="dispatch",
    )(route_i)


def _row_copy(src, dst, sem, r_src, r_dst):
    return pltpu.make_async_copy(src.at[pl.ds(r_src, 1), :], dst.at[pl.ds(r_dst, 1), :], sem)


def _scatter_kernel(dest_ref, nvalid_ref, x_ref, xs_out, zbuf, sem, zsem, *, rows):
    tm = x_ref.shape[0]
    base = pl.program_id(0) * (tm * TOP_K)

    @pl.when(pl.program_id(0) == 0)
    def _():
        zbuf[...] = jnp.zeros_like(zbuf)
        nblocks = xs_out.shape[0] // rows
        zero_copy = lambda j: pltpu.make_async_copy(zbuf, xs_out.at[pl.ds(j * rows, rows), :], zsem)

        def start(j, _):
            @pl.when(nvalid_ref[j] < rows)
            def _():
                zero_copy(j).start()
            return 0

        def wait(j, _):
            @pl.when(nvalid_ref[j] < rows)
            def _():
                zero_copy(j).wait()
            return 0

        lax.fori_loop(0, nblocks, start, 0)
        lax.fori_loop(0, nblocks, wait, 0)

    def issue(r, _):
        for k in range(TOP_K):
            _row_copy(x_ref, xs_out, sem, r, dest_ref[base + r * TOP_K + k]).start()
        return 0

    lax.fori_loop(0, tm, issue, 0, unroll=8)

    def drain(r, _):
        for k in range(TOP_K):
            _row_copy(x_ref, xs_out, sem, 0, 0).wait()
        return 0

    lax.fori_loop(0, tm, drain, 0, unroll=8)


def _scatter(dest, nvalid, x1p, rows, tm=256):
    M, W = x1p.shape
    return pl.pallas_call(
        functools.partial(_scatter_kernel, rows=rows),
        grid_spec=pltpu.PrefetchScalarGridSpec(
            num_scalar_prefetch=2,
            grid=(M // tm,),
            in_specs=[pl.BlockSpec((tm, W), lambda i, d, n: (i, 0))],
            out_specs=pl.BlockSpec(memory_space=pl.ANY),
            scratch_shapes=[pltpu.VMEM((rows, W), x1p.dtype),
                            pltpu.SemaphoreType.DMA(()), pltpu.SemaphoreType.DMA(())]),
        out_shape=jax.ShapeDtypeStruct((nvalid.shape[0] * rows, W), x1p.dtype),
        compiler_params=_cparams(("arbitrary",)),
        name="scatter",
    )(dest, nvalid, x1p)


def _experts_kernel(blk_ref, used_ref, xs_ref, wg_ref, wu_ref, wd_ref, o_ref, wg_s, wu_s, wd_s):
    j = pl.program_id(0)
    active = j < used_ref[0]

    @pl.when(active & ((j == 0) | (blk_ref[j] != blk_ref[jnp.maximum(j - 1, 0)])))
    def _():
        wg_s[...] = wg_ref[...].astype(BF16)
        wu_s[...] = wu_ref[...].astype(BF16)
        wd_s[...] = wd_ref[...].astype(BF16)

    @pl.when(active)
    def _():
        lo, hi = _unpack_bf16_pairs(xs_ref[...])
        xb = jnp.concatenate([lo.astype(BF16), hi.astype(BF16)], axis=1)
        g = jnp.dot(xb, wg_s[...], preferred_element_type=F32)
        u = jnp.dot(xb, wu_s[...], preferred_element_type=F32)
        hmid = (g * jax.nn.sigmoid(g) * u).astype(BF16)
        o_ref[...] = _pack_bf16_pairs(jnp.dot(hmid, wd_s[...], preferred_element_type=F32))

    @pl.when(j >= used_ref[0])
    def _():
        o_ref[...] = jnp.zeros_like(o_ref)


def _experts(blk_e, used, xs, wg, wu, wd, layer, rows):
    P, W = xs.shape
    D, De = wg.shape[2], wg.shape[3]
    nb = P // rows
    wspec = lambda a, b_: pl.BlockSpec((None, None, a, b_), lambda j, b, u: (layer, b[j], 0, 0))
    return pl.pallas_call(
        _experts_kernel,
        grid_spec=pltpu.PrefetchScalarGridSpec(
            num_scalar_prefetch=2,
            grid=(nb,),
            in_specs=[pl.BlockSpec((rows, W), lambda j, b, u: (j, 0)),
                      wspec(D, De), wspec(D, De), wspec(De, D)],
            out_specs=pl.BlockSpec((rows, W), lambda j, b, u: (j, 0)),
            scratch_shapes=[pltpu.VMEM((D, De), BF16), pltpu.VMEM((D, De), BF16), pltpu.VMEM((De, D), BF16)]),
        out_shape=jax.ShapeDtypeStruct((P, W), U32),
        compiler_params=_cparams(("arbitrary",)),
        name="experts",
    )(blk_e, used, xs, wg, wu, wd)


def _post_kernel(dest_ref, x1_ref, rw_ref, p_ref, os_hbm, g2_ref, b2_ref, wpg_ref, bpg_ref, wpp_ref, pg_ref,
                 out_ref, gbuf, sem, *, alpha):
    i = pl.program_id(0)
    tm = x1_ref.shape[0]
    slot = lax.rem(i, 2)

    def start_gather(tile, slot_):
        base = tile * (tm * TOP_K)

        def issue(r, _):
            for k in range(TOP_K):
                _row_copy(os_hbm, gbuf.at[slot_, k], sem.at[slot_], dest_ref[base + r * TOP_K + k], r).start()
            return 0

        lax.fori_loop(0, tm, issue, 0, unroll=8)

    @pl.when(i == 0)
    def _():
        start_gather(0, 0)

    @pl.when(i + 1 < pl.num_programs(0))
    def _():
        start_gather(i + 1, 1 - slot)

    def drain(r, _):
        for k in range(TOP_K):
            _row_copy(os_hbm, gbuf.at[slot, k], sem.at[slot], 0, 0).wait()
        return 0

    lax.fori_loop(0, tm, drain, 0, unroll=8)

    rw = rw_ref[...]
    lo0, hi0 = _unpack_bf16_pairs(gbuf[slot, 0])
    lo1, hi1 = _unpack_bf16_pairs(gbuf[slot, 1])
    w0, w1 = rw[:, 0:1], rw[:, 1:2]
    ffn = jnp.concatenate([w0 * lo0 + w1 * lo1, w0 * hi0 + w1 * hi1], axis=1)
    x2 = _layer_norm(alpha * x1_ref[...] + ffn, g2_ref[...], b2_ref[...])
    gate = jax.nn.sigmoid(jnp.dot(x2.astype(BF16), wpg_ref[...], preferred_element_type=F32) + bpg_ref[...])
    pp = jnp.dot(p_ref[...].astype(BF16), wpp_ref[...], preferred_element_type=F32)
    ple = pp * lax.rsqrt(jnp.mean(pp * pp, axis=-1, keepdims=True) + EPS) * pg_ref[...]
    out_ref[...] = x2 + gate * ple


def _post(dest, x1, route_w, p3d, layer, os, g2, b2, wpg, bpg, wpp, pg, alpha, tm=256):
    M, D = x1.shape
    Dp = p3d.shape[2]
    tile = lambda w: pl.BlockSpec((tm, w), lambda i, d: (i, 0))
    full = lambda a: pl.BlockSpec(a.shape, lambda i, d: (0,) * a.ndim)
    return pl.pallas_call(
        functools.partial(_post_kernel, alpha=alpha),
        grid_spec=pltpu.PrefetchScalarGridSpec(
            num_scalar_prefetch=1,
            grid=(M // tm,),
            in_specs=[tile(D), tile(V7X_LANES), pl.BlockSpec((None, tm, Dp), lambda i, d: (layer, i, 0)),
                      pl.BlockSpec(memory_space=pl.ANY),
                      full(g2), full(b2), full(wpg), full(bpg), full(wpp), full(pg)],
            out_specs=tile(D),
            scratch_shapes=[pltpu.VMEM((2, TOP_K, tm, D // 2), U32), pltpu.SemaphoreType.DMA((2,))]),
        out_shape=jax.ShapeDtypeStruct((M, D), F32),
        compiler_params=_cparams(("arbitrary",)),
        name="post",
    )(dest, x1, route_w, p3d, os, g2, b2, wpg, bpg, wpp, pg)


def kernel(x, p, w_in, conv_w, conv_b, w_q, w_k, b_i, b_f, mh_g, lam_re, lam_im, log_dt, b_re, b_im, c_re, c_im, d_skip, w_glu, b_glu, s5_g, w_out, ln1_g, ln1_b, w_grp, b_grp, w_rt, b_rt, w_eg, w_eu, w_ed, ln2_g, ln2_b, w_pg, b_pg, w_pp, ple_g):
    B, S, D = x.shape
    depth = w_in.shape[0]
    H, Dh = w_q.shape[1], w_q.shape[2]
    Dm = H * Dh
    G, Cn = d_skip.shape[1], d_skip.shape[2]
    Ds = G * Cn
    n_grp = w_grp.shape[-1]
    n_exp = w_eg.shape[1]
    epg = n_exp // n_grp
    M = B * S
    A = M * TOP_K
    alpha = (2 * depth) ** 0.25
    NC = S // MLSTM_CHUNK
    nblocks = A // MOE_ROWS + n_exp
    nbp = -(-nblocks // V7X_SUBLANES) * V7X_SUBLANES
    row2 = lambda a: a.reshape(1, -1).astype(F32)

    s5_all = jax.vmap(functools.partial(_s5_operators, Lc=S5_CHUNK))(
        lam_re, lam_im, log_dt, b_re, b_im, c_re, c_im)

    xc = x.reshape(M, D).astype(F32)
    for l in range(depth):
        wi = w_in[l]
        w_main = jnp.concatenate([wi[:, :3 * Dm], wi[:, 3 * Dm + 2 * H:]], axis=1).astype(BF16)
        w_gate = jnp.pad(wi[:, 3 * Dm:3 * Dm + 2 * H], ((0, 0), (0, V7X_LANES - 2 * H))).astype(BF16)
        w_router = jnp.pad(jnp.concatenate([w_grp[l], w_rt[l]], axis=1).astype(F32),
                           ((0, 0), (0, V7X_LANES - n_grp - n_exp)))
        w_router_hi = w_router.astype(BF16)
        w_router_lo = (w_router - w_router_hi.astype(F32)).astype(BF16)
        b_router = jnp.pad(jnp.concatenate([b_grp[l], b_rt[l]]).astype(F32),
                           (0, V7X_LANES - n_grp - n_exp)).reshape(1, -1)

        z_main, z_gate = _inproj(xc, w_main, w_gate)

        gates = z_gate[:, :2 * H].reshape(B, NC, MLSTM_CHUNK, 2, H).transpose(0, 4, 3, 1, 2)
        y_m = _mlstm(z_main.reshape(B, S, -1), gates, jnp.stack([b_i[l], b_f[l]]).astype(F32),
                     conv_w[l].astype(F32), row2(conv_b[l]), w_q[l].astype(BF16), w_k[l].astype(BF16),
                     row2(mh_g[l]), H, Dh).reshape(M, Dm)

        y_s = _s5(z_main.reshape(B, S, -1), 3 * Dm, *s5_all, l, Cn, S5_CHUNK).reshape(M, Ds)

        wo = w_out[l].astype(BF16)
        x1, x1p, route_i, route_w = _mixout(y_m, y_s, z_main, 3 * Dm, row2(d_skip[l]), xc,
                                       w_glu[l].astype(BF16), row2(b_glu[l]), row2(s5_g[l]),
                                       wo[:Dm], wo[Dm:], row2(ln1_g[l]), row2(ln1_b[l]),
                                       w_router_hi, w_router_lo, b_router, alpha, n_grp, epg)

        dest_l, blk_l = _dispatch(route_i, n_exp, MOE_ROWS, nbp)
        dest = dest_l[:, :TOP_K].reshape(A)
        xs = _scatter(dest, blk_l[:nblocks, 2], x1p, MOE_ROWS)
        os = _experts(blk_l[:nblocks, 0], blk_l[:1, 1], xs, w_eg, w_eu, w_ed, l, MOE_ROWS)
        xc = _post(dest, x1, route_w, p.reshape(depth, M, -1), l, os, row2(ln2_g[l]), row2(ln2_b[l]),
                   w_pg[l].astype(BF16), row2(b_pg[l]), w_pp[l].astype(BF16), row2(ple_g[l]), alpha)
    return xc.reshape(B, S, D).astype(x.dtype)
```

```python
import functools
import math

import jax
import jax.numpy as jnp
from jax import lax
from jax.experimental import pallas as pl
from jax.experimental.pallas import tpu as pltpu

F32 = jnp.float32
BF16 = jnp.bfloat16
I32 = jnp.int32
HIGHEST = lax.Precision.HIGHEST

EPS = 1e-5
V7X_LANES = 128
V7X_SUBLANES = 8
VMEM_LIMIT = 56 * 1024 * 1024

MLSTM_CHUNK = 256
S5_CHUNK = 16
MOE_ROWS = 256
TOP_K = 2


def _cparams(sem):
    return pltpu.CompilerParams(dimension_semantics=sem, vmem_limit_bytes=VMEM_LIMIT)


def _iota(shape, axis):
    return lax.broadcasted_iota(I32, shape, axis)


U32 = jnp.uint32
_HI16 = 0xFFFF0000


def _pack_bf16_pairs(v):
    W = v.shape[1] // 2
    bits = lax.bitcast_convert_type(v.astype(BF16).astype(F32), U32)
    return (bits[:, :W] >> 16) | (bits[:, W:] & U32(_HI16))


def _unpack_bf16_pairs(w):
    return lax.bitcast_convert_type(w << 16, F32), lax.bitcast_convert_type(w & U32(_HI16), F32)


def _inproj_kernel(x_ref, wm_ref, wg_ref, zm_ref, zg_ref):
    xb = x_ref[...].astype(BF16)
    zm_ref[...] = jnp.dot(xb, wm_ref[...], preferred_element_type=F32)
    zg_ref[...] = jnp.dot(xb, wg_ref[...], preferred_element_type=F32)


def _inproj(x2d, w_main, w_gate, tm=512):
    M, D = x2d.shape
    N = w_main.shape[1]
    return pl.pallas_call(
        _inproj_kernel,
        grid=(M // tm,),
        in_specs=[pl.BlockSpec((tm, D), lambda i: (i, 0)),
                  pl.BlockSpec((D, N), lambda i: (0, 0)),
                  pl.BlockSpec((D, V7X_LANES), lambda i: (0, 0))],
        out_specs=[pl.BlockSpec((tm, N), lambda i: (i, 0)),
                   pl.BlockSpec((tm, V7X_LANES), lambda i: (i, 0))],
        out_shape=[jax.ShapeDtypeStruct((M, N), F32),
                   jax.ShapeDtypeStruct((M, V7X_LANES), F32)],
        compiler_params=_cparams(("parallel",)),
        name="inproj",
    )(x2d, w_main, w_gate)


def _mlstm_kernel(bif_ref, um_ref, v_ref, o_ref, gt_ref, cw_ref, cb_ref, wq_ref, wk_ref, mhg_ref,
                  y_ref, q_s, k_s, bcum_s, an_s, r_s, g_s, ml_s):
    h = pl.program_id(1)
    S, Dh = um_ref.shape
    NC, L = gt_ref.shape[1], gt_ref.shape[2]

    u = um_ref[...]
    row = _iota((S, Dh), 0)
    width = cw_ref.shape[0]
    acc = cb_ref[...] + cw_ref[width - 1:width, :] * u
    for j in range(width - 1):
        sh = width - 1 - j
        us = jnp.where(row >= sh, pltpu.roll(u, sh, axis=0), 0.0)
        acc = acc + cw_ref[j:j + 1, :] * us
    cb = (acc * jax.nn.sigmoid(acc)).astype(BF16)
    q_s[...] = jnp.dot(cb, wq_ref[...], preferred_element_type=F32)
    k_s[...] = jnp.dot(cb, wk_ref[...], preferred_element_type=F32) * (Dh ** -0.5)

    ig = gt_ref[0] + bif_ref[0, h]
    fp = gt_ref[1] + bif_ref[1, h]
    lf = jnp.minimum(fp, 0.0) - jnp.log1p(jnp.exp(-jnp.abs(fp)))
    tri = jnp.where(_iota((L, L), 0) <= _iota((L, L), 1), 1.0, 0.0).astype(F32)
    bcum = jnp.dot(lf, tri, precision=HIGHEST, preferred_element_type=F32)
    g = bcum[:, L - 1:L]
    a = g - bcum + ig
    m_loc = jnp.max(a, axis=-1, keepdims=True)
    bcum_s[...] = bcum
    an_s[...] = a - m_loc
    r_s[...] = ig - bcum
    g_s[...] = jnp.broadcast_to(g, (NC, V7X_LANES))
    ml_s[...] = jnp.broadcast_to(m_loc, (NC, V7X_LANES))

    eye = _iota((L, L), 0) == _iota((L, L), 1)
    causal = _iota((L, L), 0) >= _iota((L, L), 1)

    def to_col(row_vec):
        return jnp.sum(jnp.where(eye, row_vec, 0.0), axis=1, keepdims=True)

    def chunk(c, carry):
        C, n, m = carry
        t0 = pl.multiple_of(c * L, L)
        b_col = to_col(bcum_s[pl.ds(c, 1), :])
        an_col = to_col(an_s[pl.ds(c, 1), :])
        r_row = r_s[pl.ds(c, 1), :]
        g_c = g_s[pl.ds(c, 1), :][:, 0:1]
        ml_c = ml_s[pl.ds(c, 1), :][:, 0:1]
        qc = q_s[pl.ds(t0, L), :]
        kc = k_s[pl.ds(t0, L), :]
        vb = v_ref[pl.ds(t0, L), :].astype(BF16)
        qb = qc.astype(BF16)

        log_d = jnp.where(causal, b_col + r_row, -jnp.inf)
        log_inter = b_col + m
        m_t = jnp.maximum(log_inter, jnp.max(log_d, axis=1, keepdims=True))
        s = lax.dot_general(qb, kc.astype(BF16), (((1,), (1,)), ((), ())),
                            preferred_element_type=F32) * jnp.exp(log_d - m_t)
        inter = jnp.exp(log_inter - m_t)
        num = (jnp.dot(s.astype(BF16), vb, preferred_element_type=F32)
               + inter * jnp.dot(qb, C.astype(BF16), preferred_element_type=F32))
        den = jnp.sum(s, axis=1, keepdims=True) + inter * jnp.sum(qc * n, axis=1, keepdims=True)
        den = jnp.maximum(jnp.abs(den), jnp.exp(-m_t))
        hh = jax.nn.sigmoid(o_ref[pl.ds(t0, L), :]) * (num / den)
        mu = jnp.mean(hh, axis=-1, keepdims=True)
        hc = hh - mu
        var = jnp.mean(hc * hc, axis=-1, keepdims=True)
        y_ref[pl.ds(t0, L), :] = hc * lax.rsqrt(var + EPS) * mhg_ref[...]

        wk = jnp.exp(an_col) * kc
        Cc = lax.dot_general(wk.astype(BF16), vb, (((0,), (0,)), ((), ())), preferred_element_type=F32)
        nc = jnp.sum(wk, axis=0, keepdims=True)
        m_new = jnp.maximum(g_c + m, ml_c)
        s_old = jnp.exp(g_c + m - m_new)
        s_new = jnp.exp(ml_c - m_new)
        return s_old * C + s_new * Cc, s_old * n + s_new * nc, m_new

    init = (jnp.zeros((Dh, Dh), F32), jnp.zeros((1, Dh), F32), jnp.zeros((1, 1), F32))
    lax.fori_loop(0, NC, chunk, init, unroll=2)


def _mlstm(z3, gates, b_if, conv_w, conv_b, wq, wk, mh_g, H, Dh):
    B, S, _ = z3.shape
    NC, L = gates.shape[3], gates.shape[4]
    blk = lambda off: pl.BlockSpec((None, S, Dh), lambda b, h: (b, 0, off + h))
    return pl.pallas_call(
        _mlstm_kernel,
        grid=(B, H),
        in_specs=[pl.BlockSpec(memory_space=pltpu.SMEM),
                  blk(0), blk(H), blk(2 * H),
                  pl.BlockSpec((None, None, 2, NC, L), lambda b, h: (b, h, 0, 0, 0)),
                  pl.BlockSpec((conv_w.shape[0], Dh), lambda b, h: (0, h)),
                  pl.BlockSpec((1, Dh), lambda b, h: (0, h)),
                  pl.BlockSpec((None, Dh, Dh), lambda b, h: (h, 0, 0)),
                  pl.BlockSpec((None, Dh, Dh), lambda b, h: (h, 0, 0)),
                  pl.BlockSpec((1, Dh), lambda b, h: (0, h))],
        out_specs=pl.BlockSpec((None, S, Dh), lambda b, h: (b, 0, h)),
        out_shape=jax.ShapeDtypeStruct((B, S, H * Dh), F32),
        scratch_shapes=[pltpu.VMEM((S, Dh), F32), pltpu.VMEM((S, Dh), F32),
                        pltpu.VMEM((NC, L), F32), pltpu.VMEM((NC, L), F32), pltpu.VMEM((NC, L), F32),
                        pltpu.VMEM((NC, V7X_LANES), F32), pltpu.VMEM((NC, V7X_LANES), F32)],
        compiler_params=_cparams(("parallel", "parallel")),
        name="mlstm",
    )(b_if, z3, z3, z3, gates, conv_w, conv_b, wq, wk, mh_g)


def _s5_operators(lam_re, lam_im, log_dt, b_re, b_im, c_re, c_im, Lc):
    G, P = lam_re.shape
    Cn = b_re.shape[-1]
    lr, li = lam_re.astype(F32), lam_im.astype(F32)
    dt = jnp.exp(log_dt.astype(F32))[:, None]
    er = jnp.exp(lr * dt)
    ar, ai = er * jnp.cos(li * dt), er * jnp.sin(li * dt)
    mag2 = lr * lr + li * li
    xr, xi = ar - 1.0, ai
    cr = (xr * lr + xi * li) / mag2
    ci = (xi * lr - xr * li) / mag2
    bbr = cr[..., None] * b_re - ci[..., None] * b_im
    bbi = cr[..., None] * b_im + ci[..., None] * b_re
    pr, pi = [jnp.ones_like(ar)], [jnp.zeros_like(ai)]
    for _ in range(Lc):
        pr.append(pr[-1] * ar - pi[-1] * ai)
        pi.append(pr[-2] * ai + pi[-1] * ar)
    pwr, pwi = jnp.stack(pr, -1), jnp.stack(pi, -1)

    rev_r = pwr[:, :, Lc - 1::-1] if Lc > 1 else pwr[:, :, :1]
    rev_i = pwi[:, :, Lc - 1::-1] if Lc > 1 else pwi[:, :, :1]
    wr = jnp.einsum('gpt,gpc->gtcp', rev_r, bbr) - jnp.einsum('gpt,gpc->gtcp', rev_i, bbi)
    wi = jnp.einsum('gpt,gpc->gtcp', rev_r, bbi) + jnp.einsum('gpt,gpc->gtcp', rev_i, bbr)
    w2 = jnp.concatenate([wr, wi, wi, wr], axis=-1).reshape(G, Lc * Cn, 4 * P)

    car = c_re[..., None] * pwr[:, None] - c_im[..., None] * pwi[:, None]
    cai = c_re[..., None] * pwi[:, None] + c_im[..., None] * pwr[:, None]
    v_re = car[..., 1:].transpose(0, 2, 3, 1)
    v_im = -cai[..., 1:].transpose(0, 2, 3, 1)
    vmat = jnp.concatenate([v_re, v_im], axis=1).reshape(G, 2 * P, Lc * Cn)

    kern = (jnp.einsum('gcpk,gpd->gkdc', car[..., :Lc], bbr, precision=HIGHEST)
            - jnp.einsum('gcpk,gpd->gkdc', cai[..., :Lc], bbi, precision=HIGHEST))
    lag = jnp.arange(Lc)[None, :] - jnp.arange(Lc)[:, None]
    tm = jnp.where((lag >= 0)[None, :, :, None, None], kern[:, jnp.clip(lag, 0, Lc - 1)], 0.0)
    tmat = tm.transpose(0, 1, 3, 2, 4).reshape(G, Lc * Cn, Lc * Cn)

    pL, qL = pwr[..., Lc], pwi[..., Lc]
    pq = jnp.stack([jnp.concatenate([pL, pL], -1), jnp.concatenate([-qL, qL], -1),
                    jnp.concatenate([qL, -qL], -1)], axis=1)
    pq = jnp.concatenate([pq, jnp.zeros((G, V7X_SUBLANES - 3, 2 * P), F32)], axis=1)
    return w2.astype(BF16), tmat.astype(BF16), vmat.astype(BF16), pq


def _s5_kernel(z_ref, w2_ref, t_ref, v_ref, pq_ref, y_ref, vt_s, z1_s, z2_s, xp_s, yt_s, st_s, *, Lc, Cn):
    B, TT, LW = z_ref.shape
    GL = LW // Cn
    NCH = TT // Lc
    R = B * NCH
    P2 = xp_s.shape[1]

    @pl.when(pl.program_id(1) == 0)
    def _():
        st_s[...] = jnp.zeros_like(st_s)

    for t in range(Lc):
        a = z_ref[:, pl.ds(t, NCH, stride=Lc), :].reshape(R, LW).T
        for g in range(GL):
            vt_s[g, Cn * t:Cn * (t + 1), :] = a[Cn * g:Cn * (g + 1), :]

    for g in range(GL):
        ub = vt_s[g].T.astype(BF16)
        z = jnp.dot(ub, w2_ref[g], preferred_element_type=F32)
        z1_s[...] = z[:, :P2]
        z2_s[...] = z[:, P2:]
        pv, qv, q2 = pq_ref[g, 0:1, :], pq_ref[g, 1:2, :], pq_ref[g, 2:3, :]
        x, xs = st_s[g, 0], st_s[g, 1]
        for c in range(NCH):
            rows = pl.ds(c, B, stride=NCH)
            xp_s[rows, :] = x
            x, xs = x * pv + xs * qv + z1_s[rows, :], xs * pv + x * q2 + z2_s[rows, :]
        st_s[g, 0] = x
        st_s[g, 1] = xs
        y = (jnp.dot(ub, t_ref[g], preferred_element_type=F32)
             + jnp.dot(xp_s[...].astype(BF16), v_ref[g], preferred_element_type=F32))
        yt = y.T
        for t in range(Lc):
            yt_s[t, Cn * g:Cn * (g + 1), :] = yt[Cn * t:Cn * (t + 1), :]

    for t in range(Lc):
        y_ref[:, pl.ds(t, NCH, stride=Lc), :] = yt_s[t].T.reshape(B, NCH, LW)


def _s5(z3, col0, w2, tmat, vmat, pq, layer, Cn, Lc, tt=512):
    B, S, _ = z3.shape
    _, G, W, _ = w2.shape
    P2 = vmat.shape[2]
    GL = V7X_LANES // Cn
    R = B * (tt // Lc)
    wspec = lambda a: pl.BlockSpec((None, GL) + a.shape[2:], lambda j, t: (layer, j, 0, 0))
    return pl.pallas_call(
        functools.partial(_s5_kernel, Lc=Lc, Cn=Cn),
        grid=(G // GL, S // tt),
        in_specs=[pl.BlockSpec((B, tt, V7X_LANES), lambda j, t: (0, t, col0 // V7X_LANES + j)),
                  wspec(w2), wspec(tmat), wspec(vmat), wspec(pq)],
        out_specs=pl.BlockSpec((B, tt, V7X_LANES), lambda j, t: (0, t, j)),
        out_shape=jax.ShapeDtypeStruct((B, S, G * Cn), F32),
        scratch_shapes=[pltpu.VMEM((GL, W, R), F32), pltpu.VMEM((R, P2), F32), pltpu.VMEM((R, P2), F32),
                        pltpu.VMEM((R, P2), F32), pltpu.VMEM((Lc, V7X_LANES, R), F32),
                        pltpu.VMEM((GL, 2, B, P2), F32)],
        compiler_params=_cparams(("parallel", "arbitrary")),
        name="s5",
    )(z3, w2, tmat, vmat, pq)


def _layer_norm(v, g, b):
    mu = jnp.mean(v, axis=-1, keepdims=True)
    vc = v - mu
    var = jnp.mean(vc * vc, axis=-1, keepdims=True)
    return vc * lax.rsqrt(var + EPS) * g + b


def _mixout_kernel(ym_ref, ys_ref, us_ref, dsk_ref, x_ref, wglu_ref, bglu_ref, s5g_ref, wom_ref, wos_ref,
                   g1_ref, b1_ref, wrh_ref, wrl_ref, br_ref, x1_ref, x1p_ref, ri_ref, rw_ref,
                   *, alpha, n_grp, epg):
    ys = ys_ref[...] + dsk_ref[...] * us_ref[...]
    gy = 0.5 * ys * (1.0 + jnp.tanh(math.sqrt(2.0 / math.pi) * (ys + 0.044715 * (ys * ys * ys))))
    glu = gy * jax.nn.sigmoid(jnp.dot(gy.astype(BF16), wglu_ref[...], preferred_element_type=F32)
                              + bglu_ref[...])
    ysn = glu * lax.rsqrt(jnp.mean(glu * glu, axis=-1, keepdims=True) + EPS) * s5g_ref[...]
    mix = (jnp.dot(ym_ref[...].astype(BF16), wom_ref[...], preferred_element_type=F32)
           + jnp.dot(ysn.astype(BF16), wos_ref[...], preferred_element_type=F32))
    x1 = _layer_norm(alpha * x_ref[...] + mix, g1_ref[...], b1_ref[...])
    x1_ref[...] = x1
    x1p_ref[...] = _pack_bf16_pairs(x1)

    xh = x1.astype(BF16)
    xl = (x1 - xh.astype(F32)).astype(BF16)
    logits = (jnp.dot(xh, wrh_ref[...], preferred_element_type=F32)
              + jnp.dot(xl, wrh_ref[...], preferred_element_type=F32)
              + jnp.dot(xh, wrl_ref[...], preferred_element_type=F32) + br_ref[...])
    tm = logits.shape[0]
    lane = _iota((tm, V7X_LANES), 1)
    big = jnp.int32(V7X_LANES)
    neg = -jnp.inf
    gl = jnp.where(lane < n_grp, logits, neg)
    gmax = jnp.max(gl, axis=-1, keepdims=True)
    grp = jnp.min(jnp.where(gl == gmax, lane, big), axis=-1, keepdims=True)
    p_grp = 1.0 / jnp.sum(jnp.exp(gl - gmax), axis=-1, keepdims=True)
    lo = n_grp + grp * epg
    el = jnp.where((lane >= lo) & (lane < lo + epg), logits, neg)
    v1 = jnp.max(el, axis=-1, keepdims=True)
    i1 = jnp.min(jnp.where(el == v1, lane, big), axis=-1, keepdims=True)
    el2 = jnp.where(lane == i1, neg, el)
    v2 = jnp.max(el2, axis=-1, keepdims=True)
    i2 = jnp.min(jnp.where(el2 == v2, lane, big), axis=-1, keepdims=True)
    e2 = jnp.exp(v2 - v1)
    w1 = p_grp / (1.0 + e2)
    w2 = p_grp * e2 / (1.0 + e2)
    ri_ref[...] = jnp.where(lane == 0, i1 - n_grp, jnp.where(lane == 1, i2 - n_grp, 0))
    rw_ref[...] = jnp.where(lane == 0, w1, jnp.where(lane == 1, w2, 0.0))


def _mixout(ym, ys, z_main, us_col, dsk, x2d, wglu, bglu, s5g, wom, wos, g1, b1, wrh, wrl, br,
            alpha, n_grp, epg, tm=512):
    M, D = x2d.shape
    Dm, Ds = ym.shape[1], ys.shape[1]
    tile = lambda w: pl.BlockSpec((tm, w), lambda i: (i, 0))
    full = lambda a: pl.BlockSpec(a.shape, lambda i: (0,) * a.ndim)
    return pl.pallas_call(
        functools.partial(_mixout_kernel, alpha=alpha, n_grp=n_grp, epg=epg),
        grid=(M // tm,),
        in_specs=[tile(Dm), tile(Ds), pl.BlockSpec((tm, Ds), lambda i: (i, us_col // Ds)), full(dsk), tile(D),
                  full(wglu), full(bglu), full(s5g), full(wom), full(wos),
                  full(g1), full(b1), full(wrh), full(wrl), full(br)],
        out_specs=[tile(D), tile(D // 2), tile(V7X_LANES), tile(V7X_LANES)],
        out_shape=[jax.ShapeDtypeStruct((M, D), F32),
                   jax.ShapeDtypeStruct((M, D // 2), U32),
                   jax.ShapeDtypeStruct((M, V7X_LANES), I32),
                   jax.ShapeDtypeStruct((M, V7X_LANES), F32)],
        compiler_params=_cparams(("parallel",)),
        name="mixout",
    )(ym, ys, z_main, dsk, x2d, wglu, bglu, s5g, wom, wos, g1, b1, wrh, wrl, br)


def _dispatch_kernel(ri_ref, dest_ref, blk_ref, run_s, pst_s, *, n_exp, rows, nbp):
    ph = pl.program_id(0)
    i = pl.program_id(1)
    tm = ri_ref.shape[0]
    lane = _iota((tm, V7X_LANES), 1)
    oh0 = lane == ri_ref[:, 0:1]
    oh1 = lane == ri_ref[:, 1:2]
    cnt = jnp.where(oh0 | oh1, 1.0, 0.0).astype(F32)
    tile_cnt = jnp.sum(cnt, axis=0, keepdims=True)

    @pl.when((ph == 0) & (i == 0))
    def _():
        run_s[...] = jnp.zeros_like(run_s)

    @pl.when(ph == 0)
    def _():
        run_s[...] = run_s[...] + tile_cnt
        dest_ref[...] = jnp.zeros_like(dest_ref)

    @pl.when((ph == 1) & (i == 0))
    def _():
        counts = run_s[...]
        nblk = jnp.floor((counts + (rows - 1)) / rows)
        tri = jnp.where(_iota((V7X_LANES, V7X_LANES), 0) < _iota((V7X_LANES, V7X_LANES), 1), 1.0, 0.0)
        bstart = jnp.dot(jnp.broadcast_to(nblk, (V7X_SUBLANES, V7X_LANES)), tri.astype(F32),
                         precision=HIGHEST, preferred_element_type=F32)[0:1, :]
        pst_s[...] = bstart * rows
        bend = bstart + nblk
        j = _iota((nbp, V7X_LANES), 0).astype(F32)
        ln = _iota((nbp, V7X_LANES), 1)
        done = jnp.sum(jnp.where((ln < n_exp) & (bend <= j), 1.0, 0.0), axis=1, keepdims=True)
        blk = jnp.minimum(done, n_exp - 1.0)
        used = jnp.sum(jnp.where(ln == n_exp - 1, bend, 0.0), axis=1, keepdims=True)
        mine = ln.astype(F32) == blk
        cnt_j = jnp.sum(jnp.where(mine, counts, 0.0), axis=1, keepdims=True)
        bst_j = jnp.sum(jnp.where(mine, bstart, 0.0), axis=1, keepdims=True)
        valid = jnp.clip(cnt_j - (j[:, 0:1] - bst_j) * rows, 0.0, float(rows))
        blk_ref[...] = jnp.where(ln == 0, blk, jnp.where(ln == 1, used, jnp.where(ln == 2, valid, 0.0))
                                 ).astype(I32)
        run_s[...] = jnp.zeros_like(run_s)

    @pl.when(ph == 1)
    def _():
        lower = jnp.where(_iota((tm, tm), 0) > _iota((tm, tm), 1), 1.0, 0.0).astype(BF16)
        excl = jnp.dot(lower, cnt.astype(BF16), preferred_element_type=F32)
        base = excl + run_s[...] + pst_s[...]
        d0 = jnp.sum(jnp.where(oh0, base, 0.0), axis=1, keepdims=True)
        d1 = jnp.sum(jnp.where(oh1, base, 0.0), axis=1, keepdims=True)
        dest_ref[...] = jnp.where(lane == 0, d0, jnp.where(lane == 1, d1, 0.0)).astype(I32)
        run_s[...] = run_s[...] + tile_cnt


def _dispatch(route_i, n_exp, rows, nbp, tm=512):
    M = route_i.shape[0]
    return pl.pallas_call(
        functools.partial(_dispatch_kernel, n_exp=n_exp, rows=rows, nbp=nbp),
        grid=(2, M // tm),
        in_specs=[pl.BlockSpec((tm, V7X_LANES), lambda p, i: (i, 0))],
        out_specs=[pl.BlockSpec((tm, V7X_LANES), lambda p, i: (i * p, 0)),
                   pl.BlockSpec((nbp, V7X_LANES), lambda p, i: (0, 0))],
        out_shape=[jax.ShapeDtypeStruct((M, V7X_LANES), I32),
                   jax.ShapeDtypeStruct((nbp, V7X_LANES), I32)],
        scratch_shapes=[pltpu.VMEM((1, V7X_LANES), F32), pltpu.VMEM((1, V7X_LANES), F32)],
        compiler_params=_cparams(("arbitrary", "arbitrary")),
        name="dispatch",
    )(route_i)


def _row_copy(src, dst, sem, r_src, r_dst):
    return pltpu.make_async_copy(src.at[pl.ds(r_src, 1), :], dst.at[pl.ds(r_dst, 1), :], sem)


def _scatter_kernel(dest_ref, nvalid_ref, x_ref, xs_out, zbuf, sem, zsem, *, rows):
    tm = x_ref.shape[0]
    base = pl.program_id(0) * (tm * TOP_K)

    @pl.when(pl.program_id(0) == 0)
    def _():
        zbuf[...] = jnp.zeros_like(zbuf)
        nblocks = xs_out.shape[0] // rows
        zero_copy = lambda j: pltpu.make_async_copy(zbuf, xs_out.at[pl.ds(j * rows, rows), :], zsem)

        def start(j, _):
            @pl.when(nvalid_ref[j] < rows)
            def _():
                zero_copy(j).start()
            return 0

        def wait(j, _):
            @pl.when(nvalid_ref[j] < rows)
            def _():
                zero_copy(j).wait()
            return 0

        lax.fori_loop(0, nblocks, start, 0)
        lax.fori_loop(0, nblocks, wait, 0)

    def issue(r, _):
        for k in range(TOP_K):
            _row_copy(x_ref, xs_out, sem, r, dest_ref[base + r * TOP_K + k]).start()
        return 0

    lax.fori_loop(0, tm, issue, 0, unroll=8)

    def drain(r, _):
        for k in range(TOP_K):
            _row_copy(x_ref, xs_out, sem, 0, 0).wait()
        return 0

    lax.fori_loop(0, tm, drain, 0, unroll=8)


def _scatter(dest, nvalid, x1p, rows, tm=256):
    M, W = x1p.shape
    return pl.pallas_call(
        functools.partial(_scatter_kernel, rows=rows),
        grid_spec=pltpu.PrefetchScalarGridSpec(
            num_scalar_prefetch=2,
            grid=(M // tm,),
            in_specs=[pl.BlockSpec((tm, W), lambda i, d, n: (i, 0))],
            out_specs=pl.BlockSpec(memory_space=pl.ANY),
            scratch_shapes=[pltpu.VMEM((rows, W), x1p.dtype),
                            pltpu.SemaphoreType.DMA(()), pltpu.SemaphoreType.DMA(())]),
        out_shape=jax.ShapeDtypeStruct((nvalid.shape[0] * rows, W), x1p.dtype),
        compiler_params=_cparams(("arbitrary",)),
        name="scatter",
    )(dest, nvalid, x1p)


def _experts_kernel(blk_ref, used_ref, xs_ref, wg_hbm, wu_hbm, wd_hbm, o_ref,
                    fg, fu, fd, wg_s, wu_s, wd_s, slot_s, wsem, *, layer):
    j = pl.program_id(0)
    used = used_ref[0]
    nblk = pl.num_programs(0)
    active = j < used
    e = blk_ref[j]

    def fetch(expert, slot):
        return [pltpu.make_async_copy(w.at[layer, expert], f.at[slot], wsem.at[slot, t])
                for t, (w, f) in enumerate(((wg_hbm, fg), (wu_hbm, fu), (wd_hbm, fd)))]

    @pl.when(active & (j == 0))
    def _():
        slot_s[0] = 0
        for c in fetch(e, 0):
            c.start()

    @pl.when(active & ((j == 0) | (e != blk_ref[jnp.maximum(j - 1, 0)])))
    def _():
        slot = slot_s[0]
        for c in fetch(e, slot):
            c.wait()
        wg_s[...] = fg[slot].astype(BF16)
        wu_s[...] = fu[slot].astype(BF16)
        wd_s[...] = fd[slot].astype(BF16)
        nxt = lax.while_loop(lambda t: (t < used) & (blk_ref[jnp.minimum(t, nblk - 1)] == e), lambda t: t + 1, j + 1)

        @pl.when(nxt < used)
        def _():
            for c in fetch(blk_ref[jnp.minimum(nxt, nblk - 1)], 1 - slot):
                c.start()

        slot_s[0] = 1 - slot

    @pl.when(active)
    def _():
        lo, hi = _unpack_bf16_pairs(xs_ref[...])
        xb = jnp.concatenate([lo.astype(BF16), hi.astype(BF16)], axis=1)
        g = jnp.dot(xb, wg_s[...], preferred_element_type=F32)
        u = jnp.dot(xb, wu_s[...], preferred_element_type=F32)
        hmid = (g * jax.nn.sigmoid(g) * u).astype(BF16)
        o_ref[...] = _pack_bf16_pairs(jnp.dot(hmid, wd_s[...], preferred_element_type=F32))

    @pl.when(j >= used)
    def _():
        o_ref[...] = jnp.zeros_like(o_ref)


def _experts(blk_e, used, xs, wg, wu, wd, layer, rows):
    P, W = xs.shape
    D, De = wg.shape[2], wg.shape[3]
    nb = P // rows
    hbm = pl.BlockSpec(memory_space=pl.ANY)
    return pl.pallas_call(
        functools.partial(_experts_kernel, layer=layer),
        grid_spec=pltpu.PrefetchScalarGridSpec(
            num_scalar_prefetch=2,
            grid=(nb,),
            in_specs=[pl.BlockSpec((rows, W), lambda j, b, u: (j, 0)), hbm, hbm, hbm],
            out_specs=pl.BlockSpec((rows, W), lambda j, b, u: (j, 0)),
            scratch_shapes=[pltpu.VMEM((2, D, De), F32), pltpu.VMEM((2, D, De), F32), pltpu.VMEM((2, De, D), F32),
                            pltpu.VMEM((D, De), BF16), pltpu.VMEM((D, De), BF16), pltpu.VMEM((De, D), BF16),
                            pltpu.SMEM((1,), I32), pltpu.SemaphoreType.DMA((2, 3))]),
        out_shape=jax.ShapeDtypeStruct((P, W), U32),
        compiler_params=_cparams(("arbitrary",)),
        name="experts",
    )(blk_e, used, xs, wg, wu, wd)


def _post_kernel(dest_ref, x1_ref, rw_ref, p_ref, os_hbm, g2_ref, b2_ref, wpg_ref, bpg_ref, wpp_ref, pg_ref,
                 out_ref, gbuf, sem, *, alpha):
    i = pl.program_id(0)
    last = pl.num_programs(0) - 1
    tm = x1_ref.shape[0]
    slot = lax.rem(i, 2)

    def gather_row(tile, slot_, r):
        for k in range(TOP_K):
            _row_copy(os_hbm, gbuf.at[slot_, k], sem.at[slot_], dest_ref[tile * (tm * TOP_K) + r * TOP_K + k],
                      r).start()

    def drain(slot_):
        def body(r, _):
            for k in range(TOP_K):
                _row_copy(os_hbm, gbuf.at[slot_, k], sem.at[slot_], 0, 0).wait()
            return 0

        lax.fori_loop(0, tm, body, 0, unroll=8)

    @pl.when(i == 0)
    def _():
        def first(r, _):
            gather_row(0, 0, r)
            return 0

        lax.fori_loop(0, tm, first, 0, unroll=8)

    drain(slot)
    nxt = jnp.minimum(i + 1, last)
    for r in range(tm):
        gather_row(nxt, 1 - slot, r)

    rw = rw_ref[...]
    lo0, hi0 = _unpack_bf16_pairs(gbuf[slot, 0])
    lo1, hi1 = _unpack_bf16_pairs(gbuf[slot, 1])
    w0, w1 = rw[:, 0:1], rw[:, 1:2]
    ffn = jnp.concatenate([w0 * lo0 + w1 * lo1, w0 * hi0 + w1 * hi1], axis=1)
    x2 = _layer_norm(alpha * x1_ref[...] + ffn, g2_ref[...], b2_ref[...])
    gate = jax.nn.sigmoid(jnp.dot(x2.astype(BF16), wpg_ref[...], preferred_element_type=F32) + bpg_ref[...])
    pp = jnp.dot(p_ref[...].astype(BF16), wpp_ref[...], preferred_element_type=F32)
    ple = pp * lax.rsqrt(jnp.mean(pp * pp, axis=-1, keepdims=True) + EPS) * pg_ref[...]
    out_ref[...] = x2 + gate * ple

    @pl.when(i == last)
    def _():
        drain(1 - slot)


def _post(dest, x1, route_w, p3d, layer, os, g2, b2, wpg, bpg, wpp, pg, alpha, tm=256):
    M, D = x1.shape
    Dp = p3d.shape[2]
    tile = lambda w: pl.BlockSpec((tm, w), lambda i, d: (i, 0))
    full = lambda a: pl.BlockSpec(a.shape, lambda i, d: (0,) * a.ndim)
    return pl.pallas_call(
        functools.partial(_post_kernel, alpha=alpha),
        grid_spec=pltpu.PrefetchScalarGridSpec(
            num_scalar_prefetch=1,
            grid=(M // tm,),
            in_specs=[tile(D), tile(V7X_LANES), pl.BlockSpec((None, tm, Dp), lambda i, d: (layer, i, 0)),
                      pl.BlockSpec(memory_space=pl.ANY),
                      full(g2), full(b2), full(wpg), full(bpg), full(wpp), full(pg)],
            out_specs=tile(D),
            scratch_shapes=[pltpu.VMEM((2, TOP_K, tm, D // 2), U32), pltpu.SemaphoreType.DMA((2,))]),
        out_shape=jax.ShapeDtypeStruct((M, D), F32),
        compiler_params=_cparams(("arbitrary",)),
        name="post",
    )(dest, x1, route_w, p3d, os, g2, b2, wpg, bpg, wpp, pg)


def kernel(x, p, w_in, conv_w, conv_b, w_q, w_k, b_i, b_f, mh_g, lam_re, lam_im, log_dt, b_re, b_im, c_re, c_im, d_skip, w_glu, b_glu, s5_g, w_out, ln1_g, ln1_b, w_grp, b_grp, w_rt, b_rt, w_eg, w_eu, w_ed, ln2_g, ln2_b, w_pg, b_pg, w_pp, ple_g):
    B, S, D = x.shape
    depth = w_in.shape[0]
    H, Dh = w_q.shape[1], w_q.shape[2]
    Dm = H * Dh
    G, Cn = d_skip.shape[1], d_skip.shape[2]
    Ds = G * Cn
    n_grp = w_grp.shape[-1]
    n_exp = w_eg.shape[1]
    epg = n_exp // n_grp
    M = B * S
    A = M * TOP_K
    alpha = (2 * depth) ** 0.25
    NC = S // MLSTM_CHUNK
    nblocks = A // MOE_ROWS + n_exp
    nbp = -(-nblocks // V7X_SUBLANES) * V7X_SUBLANES
    row2 = lambda a: a.reshape(1, -1).astype(F32)

    s5_all = jax.vmap(functools.partial(_s5_operators, Lc=S5_CHUNK))(
        lam_re, lam_im, log_dt, b_re, b_im, c_re, c_im)

    xc = x.reshape(M, D).astype(F32)
    for l in range(depth):
        wi = w_in[l]
        w_main = jnp.concatenate([wi[:, :3 * Dm], wi[:, 3 * Dm + 2 * H:]], axis=1).astype(BF16)
        w_gate = jnp.pad(wi[:, 3 * Dm:3 * Dm + 2 * H], ((0, 0), (0, V7X_LANES - 2 * H))).astype(BF16)
        w_router = jnp.pad(jnp.concatenate([w_grp[l], w_rt[l]], axis=1).astype(F32),
                           ((0, 0), (0, V7X_LANES - n_grp - n_exp)))
        w_router_hi = w_router.astype(BF16)
        w_router_lo = (w_router - w_router_hi.astype(F32)).astype(BF16)
        b_router = jnp.pad(jnp.concatenate([b_grp[l], b_rt[l]]).astype(F32),
                           (0, V7X_LANES - n_grp - n_exp)).reshape(1, -1)

        z_main, z_gate = _inproj(xc, w_main, w_gate)

        gates = z_gate[:, :2 * H].reshape(B, NC, MLSTM_CHUNK, 2, H).transpose(0, 4, 3, 1, 2)
        y_m = _mlstm(z_main.reshape(B, S, -1), gates, jnp.stack([b_i[l], b_f[l]]).astype(F32),
                     conv_w[l].astype(F32), row2(conv_b[l]), w_q[l].astype(BF16), w_k[l].astype(BF16),
                     row2(mh_g[l]), H, Dh).reshape(M, Dm)

        y_s = _s5(z_main.reshape(B, S, -1), 3 * Dm, *s5_all, l, Cn, S5_CHUNK).reshape(M, Ds)

        wo = w_out[l].astype(BF16)
        x1, x1p, route_i, route_w = _mixout(y_m, y_s, z_main, 3 * Dm, row2(d_skip[l]), xc,
                                       w_glu[l].astype(BF16), row2(b_glu[l]), row2(s5_g[l]),
                                       wo[:Dm], wo[Dm:], row2(ln1_g[l]), row2(ln1_b[l]),
                                       w_router_hi, w_router_lo, b_router, alpha, n_grp, epg)

        dest_l, blk_l = _dispatch(route_i, n_exp, MOE_ROWS, nbp)
        dest = dest_l[:, :TOP_K].reshape(A)
        xs = _scatter(dest, blk_l[:nblocks, 2], x1p, MOE_ROWS)
        os = _experts(blk_l[:nblocks, 0], blk_l[:1, 1], xs, w_eg, w_eu, w_ed, l, MOE_ROWS)
        xc = _post(dest, x1, route_w, p.reshape(depth, M, -1), l, os, row2(ln2_g[l]), row2(ln2_b[l]),
                   w_pg[l].astype(BF16), row2(b_pg[l]), w_pp[l].astype(BF16), row2(ple_g[l]), alpha)
    return xc.reshape(B, S, D).astype(x.dtype)
```

```python
import functools
import math

import jax
import jax.numpy as jnp
from jax import lax
from jax.experimental import pallas as pl
from jax.experimental.pallas import tpu as pltpu

F32 = jnp.float32
BF16 = jnp.bfloat16
I32 = jnp.int32
HIGHEST = lax.Precision.HIGHEST

EPS = 1e-5
V7X_LANES = 128
V7X_SUBLANES = 8
VMEM_LIMIT = 56 * 1024 * 1024

MLSTM_CHUNK = 256
S5_CHUNK = 16
MOE_ROWS = 256
TOP_K = 2


def _cparams(sem):
    return pltpu.CompilerParams(dimension_semantics=sem, vmem_limit_bytes=VMEM_LIMIT)


def _iota(shape, axis):
    return lax.broadcasted_iota(I32, shape, axis)


U32 = jnp.uint32
_HI16 = 0xFFFF0000


def _pack_bf16_pairs(v):
    W = v.shape[1] // 2
    bits = lax.bitcast_convert_type(v.astype(BF16).astype(F32), U32)
    return (bits[:, :W] >> 16) | (bits[:, W:] & U32(_HI16))


def _unpack_bf16_pairs(w):
    return lax.bitcast_convert_type(w << 16, F32), lax.bitcast_convert_type(w & U32(_HI16), F32)


def _inproj_kernel(x_ref, wm_ref, wg_ref, zm_ref, zg_ref):
    xb = x_ref[...].astype(BF16)
    zm_ref[...] = jnp.dot(xb, wm_ref[...], preferred_element_type=F32)
    zg_ref[...] = jnp.dot(xb, wg_ref[...], preferred_element_type=F32)


def _inproj(x2d, w_main, w_gate, tm=512):
    M, D = x2d.shape
    N = w_main.shape[1]
    return pl.pallas_call(
        _inproj_kernel,
        grid=(M // tm,),
        in_specs=[pl.BlockSpec((tm, D), lambda i: (i, 0)),
                  pl.BlockSpec((D, N), lambda i: (0, 0)),
                  pl.BlockSpec((D, V7X_LANES), lambda i: (0, 0))],
        out_specs=[pl.BlockSpec((tm, N), lambda i: (i, 0)),
                   pl.BlockSpec((tm, V7X_LANES), lambda i: (i, 0))],
        out_shape=[jax.ShapeDtypeStruct((M, N), F32),
                   jax.ShapeDtypeStruct((M, V7X_LANES), F32)],
        compiler_params=_cparams(("parallel",)),
        name="inproj",
    )(x2d, w_main, w_gate)


def _mlstm_kernel(bif_ref, um_ref, v_ref, o_ref, gt_ref, cw_ref, cb_ref, wq_ref, wk_ref, mhg_ref,
                  y_ref, q_s, k_s, r_s, so_s, sn_s, parts_s, st_s):
    h = pl.program_id(1)
    S, Dh = um_ref.shape
    NC, L = gt_ref.shape[1], gt_ref.shape[2]
    NQ = 4

    u = um_ref[...]
    row = _iota((S, Dh), 0)
    width = cw_ref.shape[0]
    acc = cb_ref[...] + cw_ref[width - 1:width, :] * u
    for j in range(width - 1):
        sh = width - 1 - j
        us = jnp.where(row >= sh, pltpu.roll(u, sh, axis=0), 0.0)
        acc = acc + cw_ref[j:j + 1, :] * us
    cb = (acc * jax.nn.sigmoid(acc)).astype(BF16)
    q_s[...] = jnp.dot(cb, wq_ref[...], preferred_element_type=F32).astype(BF16)
    k_s[...] = jnp.dot(cb, wk_ref[...], preferred_element_type=F32) * (Dh ** -0.5)

    ig = gt_ref[0] + bif_ref[0, h]
    fp = gt_ref[1] + bif_ref[1, h]
    lf = jnp.minimum(fp, 0.0) - jnp.log1p(jnp.exp(-jnp.abs(fp)))
    tri = jnp.where(_iota((L, L), 0) <= _iota((L, L), 1), 1.0, 0.0).astype(F32)
    bcum = jnp.dot(lf, tri, precision=HIGHEST, preferred_element_type=F32)
    g = bcum[:, L - 1:L]
    a = g - bcum + ig
    m_loc = jnp.max(a, axis=-1, keepdims=True)
    r = ig - bcum
    r_s[...] = r
    lane = _iota((NC, L), 1)
    rcm = r
    sh = 1
    while sh < L:
        rcm = jnp.maximum(rcm, jnp.where(lane >= sh, pltpu.roll(rcm, sh, axis=1), -jnp.inf))
        sh *= 2
    m = jnp.zeros((1, 1), F32)
    m_prev, s_old, s_new = [], [], []
    for c in range(NC):
        m_prev.append(m)
        m_new = jnp.maximum(g[c:c + 1] + m, m_loc[c:c + 1])
        s_old.append(jnp.exp(g[c:c + 1] + m - m_new))
        s_new.append(jnp.exp(m_loc[c:c + 1] - m_new))
        m = m_new
    m_prev = jnp.concatenate(m_prev, axis=0)
    so_s[...] = jnp.broadcast_to(jnp.concatenate(s_old, axis=0), (NC, 2 * Dh))
    sn_s[...] = jnp.broadcast_to(jnp.concatenate(s_new, axis=0), (NC, 2 * Dh))
    mx = jnp.maximum(m_prev, rcm)
    m_t = bcum + mx
    per_token = (-mx,
                 jnp.exp(m_prev - mx),
                 jnp.exp(-m_t),
                 jnp.exp(a - m_loc))

    kk = 4 * NQ
    er_, ec_ = _iota((kk, NQ * Dh), 0), _iota((kk, NQ * Dh), 1)
    sel = jnp.where((er_ < 3 * NQ) & (er_ % NQ == ec_ // Dh), 1.0, 0.0).astype(BF16)
    for c in range(NC):
        x4 = jnp.concatenate([p[c:c + 1] for p in per_token], axis=0)
        hi = x4.astype(BF16).astype(F32)
        mid = (x4 - hi).astype(BF16).astype(F32)
        lo = (x4 - hi - mid).astype(BF16).astype(F32)
        parts_s[c] = jnp.concatenate([hi, mid, lo, jnp.zeros_like(hi)], axis=0).astype(BF16)

    st_s[...] = jnp.zeros_like(st_s)
    causal = _iota((L, L), 0) >= _iota((L, L), 1)
    ones_b = jnp.ones((L, Dh), BF16)

    def chunk(c, _):
        t0 = pl.multiple_of(c * L, L)
        cols = lax.dot_general(parts_s[c], sel, (((0,), (0,)), ((), ())), preferred_element_type=F32)
        bm, inter, emt, ean = (cols[:, q * Dh:(q + 1) * Dh] for q in range(NQ))
        qb = q_s[pl.ds(t0, L), :]
        kc = k_s[pl.ds(t0, L), :]
        v1 = jnp.concatenate([v_ref[pl.ds(t0, L), :].astype(BF16), ones_b], axis=1)

        log_dm = jnp.where(causal, jnp.concatenate([bm] * (L // Dh), axis=1) + r_s[pl.ds(c, 1), :], -jnp.inf)
        s = lax.dot_general(qb, kc.astype(BF16), (((1,), (1,)), ((), ())),
                            preferred_element_type=F32) * jnp.exp(log_dm)
        state = st_s[...]
        nd = (jnp.dot(s.astype(BF16), v1, preferred_element_type=F32)
              + jnp.concatenate([inter, inter], axis=1)
              * jnp.dot(qb, state.astype(BF16), preferred_element_type=F32))
        den = jnp.maximum(jnp.abs(nd[:, Dh:]), emt)
        hh = jax.nn.sigmoid(o_ref[pl.ds(t0, L), :]) * (nd[:, :Dh] / den)
        mu = jnp.mean(hh, axis=-1, keepdims=True)
        hc = hh - mu
        var = jnp.mean(hc * hc, axis=-1, keepdims=True)
        y_ref[pl.ds(t0, L), :] = hc * lax.rsqrt(var + EPS) * mhg_ref[...]

        upd = lax.dot_general((ean * kc).astype(BF16), v1, (((0,), (0,)), ((), ())), preferred_element_type=F32)
        st_s[...] = so_s[pl.ds(c, 1), :] * state + sn_s[pl.ds(c, 1), :] * upd
        return 0

    lax.fori_loop(0, NC, chunk, 0, unroll=4)


def _mlstm(z3, gates, b_if, conv_w, conv_b, wq, wk, mh_g, H, Dh):
    B, S, _ = z3.shape
    NC, L = gates.shape[3], gates.shape[4]
    blk = lambda off: pl.BlockSpec((None, S, Dh), lambda b, h: (b, 0, off + h))
    return pl.pallas_call(
        _mlstm_kernel,
        grid=(B, H),
        in_specs=[pl.BlockSpec(memory_space=pltpu.SMEM),
                  blk(0), blk(H), blk(2 * H),
                  pl.BlockSpec((None, None, 2, NC, L), lambda b, h: (b, h, 0, 0, 0)),
                  pl.BlockSpec((conv_w.shape[0], Dh), lambda b, h: (0, h)),
                  pl.BlockSpec((1, Dh), lambda b, h: (0, h)),
                  pl.BlockSpec((None, Dh, Dh), lambda b, h: (h, 0, 0)),
                  pl.BlockSpec((None, Dh, Dh), lambda b, h: (h, 0, 0)),
                  pl.BlockSpec((1, Dh), lambda b, h: (0, h))],
        out_specs=pl.BlockSpec((None, S, Dh), lambda b, h: (b, 0, h)),
        out_shape=jax.ShapeDtypeStruct((B, S, H * Dh), F32),
        scratch_shapes=[pltpu.VMEM((S, Dh), BF16), pltpu.VMEM((S, Dh), F32), pltpu.VMEM((NC, L), F32),
                        pltpu.VMEM((NC, 2 * Dh), F32), pltpu.VMEM((NC, 2 * Dh), F32),
                        pltpu.VMEM((NC, 16, L), BF16), pltpu.VMEM((Dh, 2 * Dh), F32)],
        compiler_params=_cparams(("parallel", "parallel")),
        name="mlstm",
    )(b_if, z3, z3, z3, gates, conv_w, conv_b, wq, wk, mh_g)


def _s5_operators_kernel(lr_ref, li_ref, ldt_ref, b1_ref, b2_ref, b3_ref, b4_ref, c1_ref, c2_ref,
                         w2_ref, t_ref, vt_ref, pq_ref, *, Lc):
    GB, Cn, P2 = b1_ref.shape
    W = Lc * Cn
    for g in range(GB):
        lr, li = lr_ref[g], li_ref[g]
        dt = jnp.exp(ldt_ref[g])
        er = jnp.exp(lr * dt)
        ar, ai = er * jnp.cos(li * dt), er * jnp.sin(li * dt)
        mag2 = lr * lr + li * li
        xr, xi = ar - 1.0, ai
        cr = (xr * lr + xi * li) / mag2
        ci = (xi * lr - xr * li) / mag2
        bb1 = cr * b1_ref[g] + ci * b2_ref[g]
        bb2 = cr * b2_ref[g] - ci * b1_ref[g]
        bb3 = cr * b3_ref[g] + ci * b4_ref[g]
        bb4 = cr * b4_ref[g] - ci * b3_ref[g]
        c1, c2 = c1_ref[g], c2_ref[g]
        pr, pi = [jnp.ones_like(ar)], [jnp.zeros_like(ai)]
        for _ in range(Lc):
            pr.append(pr[-1] * ar - pi[-1] * ai)
            pi.append(pr[-2] * ai + pi[-1] * ar)
        ca = jnp.concatenate([pr[k] * c1 + pi[k] * c2 for k in range(Lc + 1)], axis=0)
        vt_ref[g] = ca[Cn:, :].astype(BF16)
        wa = jnp.concatenate([pr[Lc - 1 - t] * bb1 + pi[Lc - 1 - t] * bb2 for t in range(Lc)], axis=0)
        wb = jnp.concatenate([pr[Lc - 1 - t] * bb3 + pi[Lc - 1 - t] * bb4 for t in range(Lc)], axis=0)
        w2_ref[g] = jnp.concatenate([wa, wb], axis=1).astype(BF16)
        kern = lax.dot_general(bb1, ca[:W, :], (((1,), (1,)), ((), ())), precision=HIGHEST,
                               preferred_element_type=F32)
        rows = [kern] + [jnp.concatenate([jnp.zeros((Cn, Cn * sft), F32), kern[:, :W - Cn * sft]], axis=1)
                         for sft in range(1, Lc)]
        t_ref[g] = jnp.concatenate(rows, axis=0).astype(BF16)
        sgn = jnp.where(_iota((1, P2), 1) < P2 // 2, -1.0, 1.0)
        q1 = pi[Lc] * sgn
        pq_ref[g] = jnp.concatenate([pr[Lc], q1, -q1, jnp.zeros((V7X_SUBLANES - 3, P2), F32)], axis=0)


def _s5_operators(lam_re, lam_im, log_dt, b_re, b_im, c_re, c_im, Lc, gb=8):
    Dd, G, P = lam_re.shape
    Cn = b_re.shape[-1]
    W = Lc * Cn
    f = lambda a: a.astype(F32)
    dup = lambda a: jnp.concatenate([f(a), f(a)], axis=-1)[:, :, None, :]
    brt, bit = f(b_re).swapaxes(-1, -2), f(b_im).swapaxes(-1, -2)
    cat = lambda x, y: jnp.concatenate([x, y], axis=-1)
    args = (dup(lam_re), dup(lam_im), dup(jnp.broadcast_to(log_dt[..., None], lam_re.shape)),
            cat(brt, bit), cat(-bit, brt), cat(bit, brt), cat(brt, -bit),
            cat(f(c_re), -f(c_im)), cat(-f(c_im), -f(c_re)))
    spec = lambda r, c: pl.BlockSpec((None, gb, r, c), lambda d, j: (d, j, 0, 0))
    return pl.pallas_call(
        functools.partial(_s5_operators_kernel, Lc=Lc),
        grid=(Dd, G // gb),
        in_specs=[spec(1, 2 * P)] * 3 + [spec(Cn, 2 * P)] * 6,
        out_specs=[spec(W, 4 * P), spec(W, W), spec(W, 2 * P), spec(V7X_SUBLANES, 2 * P)],
        out_shape=[jax.ShapeDtypeStruct((Dd, G, W, 4 * P), BF16), jax.ShapeDtypeStruct((Dd, G, W, W), BF16),
                   jax.ShapeDtypeStruct((Dd, G, W, 2 * P), BF16),
                   jax.ShapeDtypeStruct((Dd, G, V7X_SUBLANES, 2 * P), F32)],
        compiler_params=_cparams(("parallel", "parallel")),
        name="s5_operators",
    )(*args)


def _s5_kernel(z_ref, w2_ref, t_ref, v_ref, pq_ref, y_ref, vt_s, z1_s, z2_s, xp_s, yt_s, st_s, *, Lc, Cn):
    B, TT, LW = z_ref.shape
    GL = LW // Cn
    NCH = TT // Lc
    R = B * NCH
    P2 = xp_s.shape[1]

    @pl.when(pl.program_id(1) == 0)
    def _():
        st_s[...] = jnp.zeros_like(st_s)

    for t in range(Lc):
        a = z_ref[:, pl.ds(t, NCH, stride=Lc), :].reshape(R, LW).T
        for g in range(GL):
            vt_s[g, Cn * t:Cn * (t + 1), :] = a[Cn * g:Cn * (g + 1), :]

    for g in range(GL):
        ub = vt_s[g].T.astype(BF16)
        z = jnp.dot(ub, w2_ref[g], preferred_element_type=F32)
        z1_s[...] = z[:, :P2]
        z2_s[...] = z[:, P2:]
        pv, qv, q2 = pq_ref[g, 0:1, :], pq_ref[g, 1:2, :], pq_ref[g, 2:3, :]
        x, xs = st_s[g, 0], st_s[g, 1]
        for c in range(NCH):
            rows = pl.ds(c, B, stride=NCH)
            xp_s[rows, :] = x
            x, xs = x * pv + xs * qv + z1_s[rows, :], xs * pv + x * q2 + z2_s[rows, :]
        st_s[g, 0] = x
        st_s[g, 1] = xs
        y = (jnp.dot(ub, t_ref[g], preferred_element_type=F32)
             + lax.dot_general(xp_s[...].astype(BF16), v_ref[g], (((1,), (1,)), ((), ())),
                               preferred_element_type=F32))
        yt = y.T
        for t in range(Lc):
            yt_s[t, Cn * g:Cn * (g + 1), :] = yt[Cn * t:Cn * (t + 1), :]

    for t in range(Lc):
        y_ref[:, pl.ds(t, NCH, stride=Lc), :] = yt_s[t].T.reshape(B, NCH, LW)


def _s5(z3, col0, w2, tmat, vt, pq, layer, Cn, Lc, tt=512):
    B, S, _ = z3.shape
    _, G, W, _ = w2.shape
    P2 = vt.shape[3]
    GL = V7X_LANES // Cn
    R = B * (tt // Lc)
    wspec = lambda a: pl.BlockSpec((None, GL) + a.shape[2:], lambda j, t: (layer, j, 0, 0))
    return pl.pallas_call(
        functools.partial(_s5_kernel, Lc=Lc, Cn=Cn),
        grid=(G // GL, S // tt),
        in_specs=[pl.BlockSpec((B, tt, V7X_LANES), lambda j, t: (0, t, col0 // V7X_LANES + j)),
                  wspec(w2), wspec(tmat), wspec(vt), wspec(pq)],
        out_specs=pl.BlockSpec((B, tt, V7X_LANES), lambda j, t: (0, t, j)),
        out_shape=jax.ShapeDtypeStruct((B, S, G * Cn), F32),
        scratch_shapes=[pltpu.VMEM((GL, W, R), F32), pltpu.VMEM((R, P2), F32), pltpu.VMEM((R, P2), F32),
                        pltpu.VMEM((R, P2), F32), pltpu.VMEM((Lc, V7X_LANES, R), F32),
                        pltpu.VMEM((GL, 2, B, P2), F32)],
        compiler_params=_cparams(("parallel", "arbitrary")),
        name="s5",
    )(z3, w2, tmat, vt, pq)


def _layer_norm(v, g, b):
    mu = jnp.mean(v, axis=-1, keepdims=True)
    vc = v - mu
    var = jnp.mean(vc * vc, axis=-1, keepdims=True)
    return vc * lax.rsqrt(var + EPS) * g + b


def _mixout_kernel(ym_ref, ys_ref, us_ref, dsk_ref, x_ref, wglu_ref, bglu_ref, s5g_ref, wom_ref, wos_ref,
                   g1_ref, b1_ref, wrh_ref, wrl_ref, br_ref, x1_ref, x1p_ref, ri_ref, rw_ref,
                   *, alpha, n_grp, epg):
    ys = ys_ref[...] + dsk_ref[...] * us_ref[...]
    gy = 0.5 * ys * (1.0 + jnp.tanh(math.sqrt(2.0 / math.pi) * (ys + 0.044715 * (ys * ys * ys))))
    glu = gy * jax.nn.sigmoid(jnp.dot(gy.astype(BF16), wglu_ref[...], preferred_element_type=F32)
                              + bglu_ref[...])
    ysn = glu * lax.rsqrt(jnp.mean(glu * glu, axis=-1, keepdims=True) + EPS) * s5g_ref[...]
    mix = (jnp.dot(ym_ref[...].astype(BF16), wom_ref[...], preferred_element_type=F32)
           + jnp.dot(ysn.astype(BF16), wos_ref[...], preferred_element_type=F32))
    x1 = _layer_norm(alpha * x_ref[...] + mix, g1_ref[...], b1_ref[...])
    x1_ref[...] = x1
    x1p_ref[...] = _pack_bf16_pairs(x1)

    xh = x1.astype(BF16)
    xl = (x1 - xh.astype(F32)).astype(BF16)
    logits = (jnp.dot(xh, wrh_ref[...], preferred_element_type=F32)
              + jnp.dot(xl, wrh_ref[...], preferred_element_type=F32)
              + jnp.dot(xh, wrl_ref[...], preferred_element_type=F32) + br_ref[...])
    tm = logits.shape[0]
    lane = _iota((tm, V7X_LANES), 1)
    big = jnp.int32(V7X_LANES)
    neg = -jnp.inf
    gl = jnp.where(lane < n_grp, logits, neg)
    gmax = jnp.max(gl, axis=-1, keepdims=True)
    grp = jnp.min(jnp.where(gl == gmax, lane, big), axis=-1, keepdims=True)
    p_grp = 1.0 / jnp.sum(jnp.exp(gl - gmax), axis=-1, keepdims=True)
    lo = n_grp + grp * epg
    el = jnp.where((lane >= lo) & (lane < lo + epg), logits, neg)
    v1 = jnp.max(el, axis=-1, keepdims=True)
    i1 = jnp.min(jnp.where(el == v1, lane, big), axis=-1, keepdims=True)
    el2 = jnp.where(lane == i1, neg, el)
    v2 = jnp.max(el2, axis=-1, keepdims=True)
    i2 = jnp.min(jnp.where(el2 == v2, lane, big), axis=-1, keepdims=True)
    e2 = jnp.exp(v2 - v1)
    w1 = p_grp / (1.0 + e2)
    w2 = p_grp * e2 / (1.0 + e2)
    ri_ref[...] = jnp.where(lane == 0, i1 - n_grp, jnp.where(lane == 1, i2 - n_grp, 0))
    rw_ref[...] = jnp.where(lane == 0, w1, jnp.where(lane == 1, w2, 0.0))


def _mixout(ym, ys, z_main, us_col, dsk, x2d, wglu, bglu, s5g, wom, wos, g1, b1, wrh, wrl, br,
            alpha, n_grp, epg, tm=512):
    M, D = x2d.shape
    Dm, Ds = ym.shape[1], ys.shape[1]
    tile = lambda w: pl.BlockSpec((tm, w), lambda i: (i, 0))
    full = lambda a: pl.BlockSpec(a.shape, lambda i: (0,) * a.ndim)
    return pl.pallas_call(
        functools.partial(_mixout_kernel, alpha=alpha, n_grp=n_grp, epg=epg),
        grid=(M // tm,),
        in_specs=[tile(Dm), tile(Ds), pl.BlockSpec((tm, Ds), lambda i: (i, us_col // Ds)), full(dsk), tile(D),
                  full(wglu), full(bglu), full(s5g), full(wom), full(wos),
                  full(g1), full(b1), full(wrh), full(wrl), full(br)],
        out_specs=[tile(D), tile(D // 2), tile(V7X_LANES), tile(V7X_LANES)],
        out_shape=[jax.ShapeDtypeStruct((M, D), F32),
                   jax.ShapeDtypeStruct((M, D // 2), U32),
                   jax.ShapeDtypeStruct((M, V7X_LANES), I32),
                   jax.ShapeDtypeStruct((M, V7X_LANES), F32)],
        compiler_params=_cparams(("parallel",)),
        name="mixout",
    )(ym, ys, z_main, dsk, x2d, wglu, bglu, s5g, wom, wos, g1, b1, wrh, wrl, br)


def _dispatch_kernel(ri_ref, dest_ref, blk_ref, run_s, pst_s, *, n_exp, rows, nbp):
    ph = pl.program_id(0)
    i = pl.program_id(1)
    tm = ri_ref.shape[0]
    lane = _iota((tm, V7X_LANES), 1)
    oh0 = lane == ri_ref[:, 0:1]
    oh1 = lane == ri_ref[:, 1:2]
    cnt = jnp.where(oh0 | oh1, 1.0, 0.0).astype(F32)
    tile_cnt = jnp.sum(cnt, axis=0, keepdims=True)

    @pl.when((ph == 0) & (i == 0))
    def _():
        run_s[...] = jnp.zeros_like(run_s)

    @pl.when(ph == 0)
    def _():
        run_s[...] = run_s[...] + tile_cnt
        dest_ref[...] = jnp.zeros_like(dest_ref)

    @pl.when((ph == 1) & (i == 0))
    def _():
        counts = run_s[...]
        nblk = jnp.floor((counts + (rows - 1)) / rows)
        tri = jnp.where(_iota((V7X_LANES, V7X_LANES), 0) < _iota((V7X_LANES, V7X_LANES), 1), 1.0, 0.0)
        bstart = jnp.dot(jnp.broadcast_to(nblk, (V7X_SUBLANES, V7X_LANES)), tri.astype(F32),
                         precision=HIGHEST, preferred_element_type=F32)[0:1, :]
        pst_s[...] = bstart * rows
        bend = bstart + nblk
        j = _iota((nbp, V7X_LANES), 0).astype(F32)
        ln = _iota((nbp, V7X_LANES), 1)
        done = jnp.sum(jnp.where((ln < n_exp) & (bend <= j), 1.0, 0.0), axis=1, keepdims=True)
        blk = jnp.minimum(done, n_exp - 1.0)
        used = jnp.sum(jnp.where(ln == n_exp - 1, bend, 0.0), axis=1, keepdims=True)
        mine = ln.astype(F32) == blk
        cnt_j = jnp.sum(jnp.where(mine, counts, 0.0), axis=1, keepdims=True)
        bst_j = jnp.sum(jnp.where(mine, bstart, 0.0), axis=1, keepdims=True)
        valid = jnp.clip(cnt_j - (j[:, 0:1] - bst_j) * rows, 0.0, float(rows))
        blk_ref[...] = jnp.where(ln == 0, blk, jnp.where(ln == 1, used, jnp.where(ln == 2, valid, 0.0))
                                 ).astype(I32)
        run_s[...] = jnp.zeros_like(run_s)

    @pl.when(ph == 1)
    def _():
        lower = jnp.where(_iota((tm, tm), 0) > _iota((tm, tm), 1), 1.0, 0.0).astype(BF16)
        excl = jnp.dot(lower, cnt.astype(BF16), preferred_element_type=F32)
        base = excl + run_s[...] + pst_s[...]
        d0 = jnp.sum(jnp.where(oh0, base, 0.0), axis=1, keepdims=True)
        d1 = jnp.sum(jnp.where(oh1, base, 0.0), axis=1, keepdims=True)
        dest_ref[...] = jnp.where(lane == 0, d0, jnp.where(lane == 1, d1, 0.0)).astype(I32)
        run_s[...] = run_s[...] + tile_cnt


def _dispatch(route_i, n_exp, rows, nbp, tm=512):
    M = route_i.shape[0]
    return pl.pallas_call(
        functools.partial(_dispatch_kernel, n_exp=n_exp, rows=rows, nbp=nbp),
        grid=(2, M // tm),
        in_specs=[pl.BlockSpec((tm, V7X_LANES), lambda p, i: (i, 0))],
        out_specs=[pl.BlockSpec((tm, V7X_LANES), lambda p, i: (i * p, 0)),
                   pl.BlockSpec((nbp, V7X_LANES), lambda p, i: (0, 0))],
        out_shape=[jax.ShapeDtypeStruct((M, V7X_LANES), I32),
                   jax.ShapeDtypeStruct((nbp, V7X_LANES), I32)],
        scratch_shapes=[pltpu.VMEM((1, V7X_LANES), F32), pltpu.VMEM((1, V7X_LANES), F32)],
        compiler_params=_cparams(("arbitrary", "arbitrary")),
        name="dispatch",
    )(route_i)


def _row_copy(src, dst, sem, r_src, r_dst):
    return pltpu.make_async_copy(src.at[pl.ds(r_src, 1), :], dst.at[pl.ds(r_dst, 1), :], sem)


def _scatter_kernel(dest_ref, nvalid_ref, x_ref, xs_out, zbuf, sem, zsem, *, rows):
    tm = x_ref.shape[0]
    base = pl.program_id(0) * (tm * TOP_K)

    @pl.when(pl.program_id(0) == 0)
    def _():
        zbuf[...] = jnp.zeros_like(zbuf)
        nblocks = xs_out.shape[0] // rows
        zero_copy = lambda j: pltpu.make_async_copy(zbuf, xs_out.at[pl.ds(j * rows, rows), :], zsem)

        def start(j, _):
            @pl.when(nvalid_ref[j] < rows)
            def _():
                zero_copy(j).start()
            return 0

        def wait(j, _):
            @pl.when(nvalid_ref[j] < rows)
            def _():
                zero_copy(j).wait()
            return 0

        lax.fori_loop(0, nblocks, start, 0)
        lax.fori_loop(0, nblocks, wait, 0)

    def issue(r, _):
        for k in range(TOP_K):
            _row_copy(x_ref, xs_out, sem, r, dest_ref[base + r * TOP_K + k]).start()
        return 0

    lax.fori_loop(0, tm, issue, 0, unroll=8)

    def drain(r, _):
        for k in range(TOP_K):
            _row_copy(x_ref, xs_out, sem, 0, 0).wait()
        return 0

    lax.fori_loop(0, tm, drain, 0, unroll=8)


def _scatter(dest, nvalid, x1p, rows, tm=256):
    M, W = x1p.shape
    return pl.pallas_call(
        functools.partial(_scatter_kernel, rows=rows),
        grid_spec=pltpu.PrefetchScalarGridSpec(
            num_scalar_prefetch=2,
            grid=(M // tm,),
            in_specs=[pl.BlockSpec((tm, W), lambda i, d, n: (i, 0))],
            out_specs=pl.BlockSpec(memory_space=pl.ANY),
            scratch_shapes=[pltpu.VMEM((rows, W), x1p.dtype),
                            pltpu.SemaphoreType.DMA(()), pltpu.SemaphoreType.DMA(())]),
        out_shape=jax.ShapeDtypeStruct((nvalid.shape[0] * rows, W), x1p.dtype),
        compiler_params=_cparams(("arbitrary",)),
        name="scatter",
    )(dest, nvalid, x1p)


def _experts_kernel(blk_ref, used_ref, xs_ref, wg_hbm, wu_hbm, wd_hbm, o_ref,
                    fg, fu, fd, wg_s, wu_s, wd_s, slot_s, wsem, *, layer):
    j = pl.program_id(0)
    used = used_ref[0]
    nblk = pl.num_programs(0)
    active = j < used
    e = blk_ref[j]

    def fetch(expert, slot):
        return [pltpu.make_async_copy(w.at[layer, expert], f.at[slot], wsem.at[slot, t])
                for t, (w, f) in enumerate(((wg_hbm, fg), (wu_hbm, fu), (wd_hbm, fd)))]

    @pl.when(active & (j == 0))
    def _():
        slot_s[0] = 0
        for c in fetch(e, 0):
            c.start()

    @pl.when(active & ((j == 0) | (e != blk_ref[jnp.maximum(j - 1, 0)])))
    def _():
        slot = slot_s[0]
        for c in fetch(e, slot):
            c.wait()
        wg_s[...] = fg[slot].astype(BF16)
        wu_s[...] = fu[slot].astype(BF16)
        wd_s[...] = fd[slot].astype(BF16)
        nxt = lax.while_loop(lambda t: (t < used) & (blk_ref[jnp.minimum(t, nblk - 1)] == e), lambda t: t + 1, j + 1)

        @pl.when(nxt < used)
        def _():
            for c in fetch(blk_ref[jnp.minimum(nxt, nblk - 1)], 1 - slot):
                c.start()

        slot_s[0] = 1 - slot

    @pl.when(active)
    def _():
        lo, hi = _unpack_bf16_pairs(xs_ref[...])
        xb = jnp.concatenate([lo.astype(BF16), hi.astype(BF16)], axis=1)
        g = jnp.dot(xb, wg_s[...], preferred_element_type=F32)
        u = jnp.dot(xb, wu_s[...], preferred_element_type=F32)
        hmid = (g * jax.nn.sigmoid(g) * u).astype(BF16)
        o_ref[...] = _pack_bf16_pairs(jnp.dot(hmid, wd_s[...], preferred_element_type=F32))

    @pl.when(j >= used)
    def _():
        o_ref[...] = jnp.zeros_like(o_ref)


def _experts(blk_e, used, xs, wg, wu, wd, layer, rows):
    P, W = xs.shape
    D, De = wg.shape[2], wg.shape[3]
    nb = P // rows
    hbm = pl.BlockSpec(memory_space=pl.ANY)
    return pl.pallas_call(
        functools.partial(_experts_kernel, layer=layer),
        grid_spec=pltpu.PrefetchScalarGridSpec(
            num_scalar_prefetch=2,
            grid=(nb,),
            in_specs=[pl.BlockSpec((rows, W), lambda j, b, u: (j, 0)), hbm, hbm, hbm],
            out_specs=pl.BlockSpec((rows, W), lambda j, b, u: (j, 0)),
            scratch_shapes=[pltpu.VMEM((2, D, De), F32), pltpu.VMEM((2, D, De), F32), pltpu.VMEM((2, De, D), F32),
                            pltpu.VMEM((D, De), BF16), pltpu.VMEM((D, De), BF16), pltpu.VMEM((De, D), BF16),
                            pltpu.SMEM((1,), I32), pltpu.SemaphoreType.DMA((2, 3))]),
        out_shape=jax.ShapeDtypeStruct((P, W), U32),
        compiler_params=_cparams(("arbitrary",)),
        name="experts",
    )(blk_e, used, xs, wg, wu, wd)


def _post_kernel(dest_ref, x1_ref, rw_ref, p_ref, os_hbm, g2_ref, b2_ref, wpg_ref, bpg_ref, wpp_ref, pg_ref,
                 out_ref, gbuf, sem, *, alpha):
    i = pl.program_id(0)
    last = pl.num_programs(0) - 1
    tm = x1_ref.shape[0]
    slot = lax.rem(i, 2)

    def gather_row(tile, slot_, r):
        for k in range(TOP_K):
            _row_copy(os_hbm, gbuf.at[slot_, k], sem.at[slot_], dest_ref[tile * (tm * TOP_K) + r * TOP_K + k],
                      r).start()

    def drain(slot_):
        def body(r, _):
            for k in range(TOP_K):
                _row_copy(os_hbm, gbuf.at[slot_, k], sem.at[slot_], 0, 0).wait()
            return 0

        lax.fori_loop(0, tm, body, 0, unroll=8)

    @pl.when(i == 0)
    def _():
        def first(r, _):
            gather_row(0, 0, r)
            return 0

        lax.fori_loop(0, tm, first, 0, unroll=8)

    drain(slot)
    nxt = jnp.minimum(i + 1, last)
    for r in range(tm):
        gather_row(nxt, 1 - slot, r)

    rw = rw_ref[...]
    lo0, hi0 = _unpack_bf16_pairs(gbuf[slot, 0])
    lo1, hi1 = _unpack_bf16_pairs(gbuf[slot, 1])
    w0, w1 = rw[:, 0:1], rw[:, 1:2]
    ffn = jnp.concatenate([w0 * lo0 + w1 * lo1, w0 * hi0 + w1 * hi1], axis=1)
    x2 = _layer_norm(alpha * x1_ref[...] + ffn, g2_ref[...], b2_ref[...])
    gate = jax.nn.sigmoid(jnp.dot(x2.astype(BF16), wpg_ref[...], preferred_element_type=F32) + bpg_ref[...])
    pp = jnp.dot(p_ref[...].astype(BF16), wpp_ref[...], preferred_element_type=F32)
    ple = pp * lax.rsqrt(jnp.mean(pp * pp, axis=-1, keepdims=True) + EPS) * pg_ref[...]
    out_ref[...] = x2 + gate * ple

    @pl.when(i == last)
    def _():
        drain(1 - slot)


def _post(dest, x1, route_w, p3d, layer, os, g2, b2, wpg, bpg, wpp, pg, alpha, tm=256):
    M, D = x1.shape
    Dp = p3d.shape[2]
    tile = lambda w: pl.BlockSpec((tm, w), lambda i, d: (i, 0))
    full = lambda a: pl.BlockSpec(a.shape, lambda i, d: (0,) * a.ndim)
    return pl.pallas_call(
        functools.partial(_post_kernel, alpha=alpha),
        grid_spec=pltpu.PrefetchScalarGridSpec(
            num_scalar_prefetch=1,
            grid=(M // tm,),
            in_specs=[tile(D), tile(V7X_LANES), pl.BlockSpec((None, tm, Dp), lambda i, d: (layer, i, 0)),
                      pl.BlockSpec(memory_space=pl.ANY),
                      full(g2), full(b2), full(wpg), full(bpg), full(wpp), full(pg)],
            out_specs=tile(D),
            scratch_shapes=[pltpu.VMEM((2, TOP_K, tm, D // 2), U32), pltpu.SemaphoreType.DMA((2,))]),
        out_shape=jax.ShapeDtypeStruct((M, D), F32),
        compiler_params=_cparams(("arbitrary",)),
        name="post",
    )(dest, x1, route_w, p3d, os, g2, b2, wpg, bpg, wpp, pg)


def kernel(x, p, w_in, conv_w, conv_b, w_q, w_k, b_i, b_f, mh_g, lam_re, lam_im, log_dt, b_re, b_im, c_re, c_im, d_skip, w_glu, b_glu, s5_g, w_out, ln1_g, ln1_b, w_grp, b_grp, w_rt, b_rt, w_eg, w_eu, w_ed, ln2_g, ln2_b, w_pg, b_pg, w_pp, ple_g):
    B, S, D = x.shape
    depth = w_in.shape[0]
    H, Dh = w_q.shape[1], w_q.shape[2]
    Dm = H * Dh
    G, Cn = d_skip.shape[1], d_skip.shape[2]
    Ds = G * Cn
    n_grp = w_grp.shape[-1]
    n_exp = w_eg.shape[1]
    epg = n_exp // n_grp
    M = B * S
    A = M * TOP_K
    alpha = (2 * depth) ** 0.25
    NC = S // MLSTM_CHUNK
    nblocks = A // MOE_ROWS + n_exp
    nbp = -(-nblocks // V7X_SUBLANES) * V7X_SUBLANES
    row2 = lambda a: a.reshape(1, -1).astype(F32)

    s5_all = _s5_operators(lam_re, lam_im, log_dt, b_re, b_im, c_re, c_im, S5_CHUNK)

    xc = x.reshape(M, D).astype(F32)
    for l in range(depth):
        wi = w_in[l]
        w_main = jnp.concatenate([wi[:, :3 * Dm], wi[:, 3 * Dm + 2 * H:]], axis=1).astype(BF16)
        w_gate = jnp.pad(wi[:, 3 * Dm:3 * Dm + 2 * H], ((0, 0), (0, V7X_LANES - 2 * H))).astype(BF16)
        w_router = jnp.pad(jnp.concatenate([w_grp[l], w_rt[l]], axis=1).astype(F32),
                           ((0, 0), (0, V7X_LANES - n_grp - n_exp)))
        w_router_hi = w_router.astype(BF16)
        w_router_lo = (w_router - w_router_hi.astype(F32)).astype(BF16)
        b_router = jnp.pad(jnp.concatenate([b_grp[l], b_rt[l]]).astype(F32),
                           (0, V7X_LANES - n_grp - n_exp)).reshape(1, -1)

        z_main, z_gate = _inproj(xc, w_main, w_gate)

        gates = z_gate[:, :2 * H].reshape(B, NC, MLSTM_CHUNK, 2, H).transpose(0, 4, 3, 1, 2)
        y_m = _mlstm(z_main.reshape(B, S, -1), gates, jnp.stack([b_i[l], b_f[l]]).astype(F32),
                     conv_w[l].astype(F32), row2(conv_b[l]), w_q[l].astype(BF16), w_k[l].astype(BF16),
                     row2(mh_g[l]), H, Dh).reshape(M, Dm)

        y_s = _s5(z_main.reshape(B, S, -1), 3 * Dm, *s5_all, l, Cn, S5_CHUNK).reshape(M, Ds)

        wo = w_out[l].astype(BF16)
        x1, x1p, route_i, route_w = _mixout(y_m, y_s, z_main, 3 * Dm, row2(d_skip[l]), xc,
                                       w_glu[l].astype(BF16), row2(b_glu[l]), row2(s5_g[l]),
                                       wo[:Dm], wo[Dm:], row2(ln1_g[l]), row2(ln1_b[l]),
                                       w_router_hi, w_router_lo, b_router, alpha, n_grp, epg)

        dest_l, blk_l = _dispatch(route_i, n_exp, MOE_ROWS, nbp)
        dest = dest_l[:, :TOP_K].reshape(A)
        xs = _scatter(dest, blk_l[:nblocks, 2], x1p, MOE_ROWS)
        os = _experts(blk_l[:nblocks, 0], blk_l[:1, 1], xs, w_eg, w_eu, w_ed, l, MOE_ROWS)
        xc = _post(dest, x1, route_w, p.reshape(depth, M, -1), l, os, row2(ln2_g[l]), row2(ln2_b[l]),
                   w_pg[l].astype(BF16), row2(b_pg[l]), w_pp[l].astype(BF16), row2(ple_g[l]), alpha)
    return xc.reshape(B, S, D).astype(x.dtype)
```

```python
import functools
import math

import jax
import jax.numpy as jnp
from jax import lax
from jax.experimental import pallas as pl
from jax.experimental.pallas import tpu as pltpu

F32 = jnp.float32
BF16 = jnp.bfloat16
I32 = jnp.int32
HIGHEST = lax.Precision.HIGHEST

EPS = 1e-5
V7X_LANES = 128
V7X_SUBLANES = 8
VMEM_LIMIT = 56 * 1024 * 1024

MLSTM_CHUNK = 256
S5_CHUNK = 16
MOE_ROWS = 256
TOP_K = 2


def _cparams(sem):
    return pltpu.CompilerParams(dimension_semantics=sem, vmem_limit_bytes=VMEM_LIMIT)


def _iota(shape, axis):
    return lax.broadcasted_iota(I32, shape, axis)


U32 = jnp.uint32
_HI16 = 0xFFFF0000


def _pack_bf16_pairs(v):
    W = v.shape[1] // 2
    bits = lax.bitcast_convert_type(v.astype(BF16).astype(F32), U32)
    return (bits[:, :W] >> 16) | (bits[:, W:] & U32(_HI16))


def _unpack_bf16_pairs(w):
    return lax.bitcast_convert_type(w << 16, F32), lax.bitcast_convert_type(w & U32(_HI16), F32)


TOKEN_ROWS = 4


def _store_token_rows(ref, words):
    R = words.shape[0]
    for q in range(TOKEN_ROWS):
        ref[pl.ds(q, R, stride=TOKEN_ROWS), :] = words[:, q * V7X_LANES:(q + 1) * V7X_LANES]


def _load_token_rows(ref):
    R = ref.shape[0] // TOKEN_ROWS
    return jnp.concatenate([ref[pl.ds(q, R, stride=TOKEN_ROWS), :] for q in range(TOKEN_ROWS)], axis=1)


def _inproj_kernel(x_ref, wm_ref, wg_ref, zm_ref, zg_ref):
    xb = x_ref[...].astype(BF16)
    zm_ref[...] = jnp.dot(xb, wm_ref[...], preferred_element_type=F32)
    zg_ref[...] = jnp.dot(xb, wg_ref[...], preferred_element_type=F32)


def _inproj(x2d, w_main, w_gate, tm=512):
    M, D = x2d.shape
    N = w_main.shape[1]
    return pl.pallas_call(
        _inproj_kernel,
        grid=(M // tm,),
        in_specs=[pl.BlockSpec((tm, D), lambda i: (i, 0)),
                  pl.BlockSpec((D, N), lambda i: (0, 0)),
                  pl.BlockSpec((D, V7X_LANES), lambda i: (0, 0))],
        out_specs=[pl.BlockSpec((tm, N), lambda i: (i, 0)),
                   pl.BlockSpec((tm, V7X_LANES), lambda i: (i, 0))],
        out_shape=[jax.ShapeDtypeStruct((M, N), F32),
                   jax.ShapeDtypeStruct((M, V7X_LANES), F32)],
        compiler_params=_cparams(("parallel",)),
        name="inproj",
    )(x2d, w_main, w_gate)


def _mlstm_kernel(bif_ref, um_ref, v_ref, o_ref, gt_ref, cw_ref, cb_ref, wq_ref, wk_ref, mhg_ref,
                  y_ref, q_s, k_s, r_s, so_s, sn_s, parts_s, st_s):
    h = pl.program_id(1)
    S, Dh = um_ref.shape
    NC, L = gt_ref.shape[1], gt_ref.shape[2]
    NQ = 4

    u = um_ref[...]
    row = _iota((S, Dh), 0)
    width = cw_ref.shape[0]
    acc = cb_ref[...] + cw_ref[width - 1:width, :] * u
    for j in range(width - 1):
        sh = width - 1 - j
        us = jnp.where(row >= sh, pltpu.roll(u, sh, axis=0), 0.0)
        acc = acc + cw_ref[j:j + 1, :] * us
    cb = (acc * jax.nn.sigmoid(acc)).astype(BF16)
    q_s[...] = jnp.dot(cb, wq_ref[...], preferred_element_type=F32).astype(BF16)
    k_s[...] = jnp.dot(cb, wk_ref[...], preferred_element_type=F32) * (Dh ** -0.5)

    ig = gt_ref[0] + bif_ref[0, h]
    fp = gt_ref[1] + bif_ref[1, h]
    lf = jnp.minimum(fp, 0.0) - jnp.log1p(jnp.exp(-jnp.abs(fp)))
    tri = jnp.where(_iota((L, L), 0) <= _iota((L, L), 1), 1.0, 0.0).astype(F32)
    bcum = jnp.dot(lf, tri, precision=HIGHEST, preferred_element_type=F32)
    g = bcum[:, L - 1:L]
    a = g - bcum + ig
    m_loc = jnp.max(a, axis=-1, keepdims=True)
    r = ig - bcum
    r_s[...] = r
    lane = _iota((NC, L), 1)
    rcm = r
    sh = 1
    while sh < L:
        rcm = jnp.maximum(rcm, jnp.where(lane >= sh, pltpu.roll(rcm, sh, axis=1), -jnp.inf))
        sh *= 2
    m = jnp.zeros((1, 1), F32)
    m_prev, s_old, s_new = [], [], []
    for c in range(NC):
        m_prev.append(m)
        m_new = jnp.maximum(g[c:c + 1] + m, m_loc[c:c + 1])
        s_old.append(jnp.exp(g[c:c + 1] + m - m_new))
        s_new.append(jnp.exp(m_loc[c:c + 1] - m_new))
        m = m_new
    m_prev = jnp.concatenate(m_prev, axis=0)
    so_s[...] = jnp.broadcast_to(jnp.concatenate(s_old, axis=0), (NC, 2 * Dh))
    sn_s[...] = jnp.broadcast_to(jnp.concatenate(s_new, axis=0), (NC, 2 * Dh))
    mx = jnp.maximum(m_prev, rcm)
    m_t = bcum + mx
    per_token = (-mx,
                 jnp.exp(m_prev - mx),
                 jnp.exp(-m_t),
                 jnp.exp(a - m_loc))

    kk = 4 * NQ
    er_, ec_ = _iota((kk, NQ * Dh), 0), _iota((kk, NQ * Dh), 1)
    sel = jnp.where((er_ < 3 * NQ) & (er_ % NQ == ec_ // Dh), 1.0, 0.0).astype(BF16)
    for c in range(NC):
        x4 = jnp.concatenate([p[c:c + 1] for p in per_token], axis=0)
        hi = x4.astype(BF16).astype(F32)
        mid = (x4 - hi).astype(BF16).astype(F32)
        lo = (x4 - hi - mid).astype(BF16).astype(F32)
        parts_s[c] = jnp.concatenate([hi, mid, lo, jnp.zeros_like(hi)], axis=0).astype(BF16)

    st_s[...] = jnp.zeros_like(st_s)
    causal = _iota((L, L), 0) >= _iota((L, L), 1)
    ones_b = jnp.ones((L, Dh), BF16)

    def chunk(c, _):
        t0 = pl.multiple_of(c * L, L)
        cols = lax.dot_general(parts_s[c], sel, (((0,), (0,)), ((), ())), preferred_element_type=F32)
        bm, inter, emt, ean = (cols[:, q * Dh:(q + 1) * Dh] for q in range(NQ))
        qb = q_s[pl.ds(t0, L), :]
        kc = k_s[pl.ds(t0, L), :]
        v1 = jnp.concatenate([v_ref[pl.ds(t0, L), :].astype(BF16), ones_b], axis=1)

        log_dm = jnp.where(causal, jnp.concatenate([bm] * (L // Dh), axis=1) + r_s[pl.ds(c, 1), :], -jnp.inf)
        s = lax.dot_general(qb, kc.astype(BF16), (((1,), (1,)), ((), ())),
                            preferred_element_type=F32) * jnp.exp(log_dm)
        state = st_s[...]
        nd = (jnp.dot(s.astype(BF16), v1, preferred_element_type=F32)
              + jnp.concatenate([inter, inter], axis=1)
              * jnp.dot(qb, state.astype(BF16), preferred_element_type=F32))
        den = jnp.maximum(jnp.abs(nd[:, Dh:]), emt)
        hh = jax.nn.sigmoid(o_ref[pl.ds(t0, L), :]) * (nd[:, :Dh] / den)
        mu = jnp.mean(hh, axis=-1, keepdims=True)
        hc = hh - mu
        var = jnp.mean(hc * hc, axis=-1, keepdims=True)
        y_ref[pl.ds(t0, L), :] = hc * lax.rsqrt(var + EPS) * mhg_ref[...]

        upd = lax.dot_general((ean * kc).astype(BF16), v1, (((0,), (0,)), ((), ())), preferred_element_type=F32)
        st_s[...] = so_s[pl.ds(c, 1), :] * state + sn_s[pl.ds(c, 1), :] * upd
        return 0

    lax.fori_loop(0, NC, chunk, 0, unroll=4)


def _mlstm(z3, gates, b_if, conv_w, conv_b, wq, wk, mh_g, H, Dh):
    B, S, _ = z3.shape
    NC, L = gates.shape[3], gates.shape[4]
    blk = lambda off: pl.BlockSpec((None, S, Dh), lambda b, h: (b, 0, off + h))
    return pl.pallas_call(
        _mlstm_kernel,
        grid=(B, H),
        in_specs=[pl.BlockSpec(memory_space=pltpu.SMEM),
                  blk(0), blk(H), blk(2 * H),
                  pl.BlockSpec((None, None, 2, NC, L), lambda b, h: (b, h, 0, 0, 0)),
                  pl.BlockSpec((conv_w.shape[0], Dh), lambda b, h: (0, h)),
                  pl.BlockSpec((1, Dh), lambda b, h: (0, h)),
                  pl.BlockSpec((None, Dh, Dh), lambda b, h: (h, 0, 0)),
                  pl.BlockSpec((None, Dh, Dh), lambda b, h: (h, 0, 0)),
                  pl.BlockSpec((1, Dh), lambda b, h: (0, h))],
        out_specs=pl.BlockSpec((None, S, Dh), lambda b, h: (b, 0, h)),
        out_shape=jax.ShapeDtypeStruct((B, S, H * Dh), F32),
        scratch_shapes=[pltpu.VMEM((S, Dh), BF16), pltpu.VMEM((S, Dh), F32), pltpu.VMEM((NC, L), F32),
                        pltpu.VMEM((NC, 2 * Dh), F32), pltpu.VMEM((NC, 2 * Dh), F32),
                        pltpu.VMEM((NC, 16, L), BF16), pltpu.VMEM((Dh, 2 * Dh), F32)],
        compiler_params=_cparams(("parallel", "parallel")),
        name="mlstm",
    )(b_if, z3, z3, z3, gates, conv_w, conv_b, wq, wk, mh_g)


def _s5_operators_kernel(lr_ref, li_ref, ldt_ref, b1_ref, b2_ref, b3_ref, b4_ref, c1_ref, c2_ref,
                         w2_ref, t_ref, vt_ref, pq_ref, *, Lc):
    GB, Cn, P2 = b1_ref.shape
    W = Lc * Cn
    for g in range(GB):
        lr, li = lr_ref[g], li_ref[g]
        dt = jnp.exp(ldt_ref[g])
        er = jnp.exp(lr * dt)
        ar, ai = er * jnp.cos(li * dt), er * jnp.sin(li * dt)
        mag2 = lr * lr + li * li
        xr, xi = ar - 1.0, ai
        cr = (xr * lr + xi * li) / mag2
        ci = (xi * lr - xr * li) / mag2
        bb1 = cr * b1_ref[g] + ci * b2_ref[g]
        bb2 = cr * b2_ref[g] - ci * b1_ref[g]
        bb3 = cr * b3_ref[g] + ci * b4_ref[g]
        bb4 = cr * b4_ref[g] - ci * b3_ref[g]
        c1, c2 = c1_ref[g], c2_ref[g]
        pr, pi = [jnp.ones_like(ar)], [jnp.zeros_like(ai)]
        for _ in range(Lc):
            pr.append(pr[-1] * ar - pi[-1] * ai)
            pi.append(pr[-2] * ai + pi[-1] * ar)
        ca = jnp.concatenate([pr[k] * c1 + pi[k] * c2 for k in range(Lc + 1)], axis=0)
        vt_ref[g] = ca[Cn:, :].astype(BF16)
        wa = jnp.concatenate([pr[Lc - 1 - t] * bb1 + pi[Lc - 1 - t] * bb2 for t in range(Lc)], axis=0)
        wb = jnp.concatenate([pr[Lc - 1 - t] * bb3 + pi[Lc - 1 - t] * bb4 for t in range(Lc)], axis=0)
        w2_ref[g] = jnp.concatenate([wa, wb], axis=1).astype(BF16)
        kern = lax.dot_general(bb1, ca[:W, :], (((1,), (1,)), ((), ())), precision=HIGHEST,
                               preferred_element_type=F32)
        rows = [kern] + [jnp.concatenate([jnp.zeros((Cn, Cn * sft), F32), kern[:, :W - Cn * sft]], axis=1)
                         for sft in range(1, Lc)]
        t_ref[g] = jnp.concatenate(rows, axis=0).astype(BF16)
        sgn = jnp.where(_iota((1, P2), 1) < P2 // 2, -1.0, 1.0)
        q1 = pi[Lc] * sgn
        pq_ref[g] = jnp.concatenate([pr[Lc], q1, -q1, jnp.zeros((V7X_SUBLANES - 3, P2), F32)], axis=0)


def _s5_operators(lam_re, lam_im, log_dt, b_re, b_im, c_re, c_im, Lc, gb=8):
    Dd, G, P = lam_re.shape
    Cn = b_re.shape[-1]
    W = Lc * Cn
    f = lambda a: a.astype(F32)
    dup = lambda a: jnp.concatenate([f(a), f(a)], axis=-1)[:, :, None, :]
    brt, bit = f(b_re).swapaxes(-1, -2), f(b_im).swapaxes(-1, -2)
    cat = lambda x, y: jnp.concatenate([x, y], axis=-1)
    args = (dup(lam_re), dup(lam_im), dup(jnp.broadcast_to(log_dt[..., None], lam_re.shape)),
            cat(brt, bit), cat(-bit, brt), cat(bit, brt), cat(brt, -bit),
            cat(f(c_re), -f(c_im)), cat(-f(c_im), -f(c_re)))
    spec = lambda r, c: pl.BlockSpec((None, gb, r, c), lambda d, j: (d, j, 0, 0))
    return pl.pallas_call(
        functools.partial(_s5_operators_kernel, Lc=Lc),
        grid=(Dd, G // gb),
        in_specs=[spec(1, 2 * P)] * 3 + [spec(Cn, 2 * P)] * 6,
        out_specs=[spec(W, 4 * P), spec(W, W), spec(W, 2 * P), spec(V7X_SUBLANES, 2 * P)],
        out_shape=[jax.ShapeDtypeStruct((Dd, G, W, 4 * P), BF16), jax.ShapeDtypeStruct((Dd, G, W, W), BF16),
                   jax.ShapeDtypeStruct((Dd, G, W, 2 * P), BF16),
                   jax.ShapeDtypeStruct((Dd, G, V7X_SUBLANES, 2 * P), F32)],
        compiler_params=_cparams(("parallel", "parallel")),
        name="s5_operators",
    )(*args)


def _s5_kernel(z_ref, w2_ref, t_ref, v_ref, pq_ref, y_ref, vt_s, z1_s, z2_s, xp_s, yt_s, st_s, *, Lc, Cn):
    B, TT, LW = z_ref.shape
    GL = LW // Cn
    NCH = TT // Lc
    R = B * NCH
    P2 = xp_s.shape[1]

    @pl.when(pl.program_id(1) == 0)
    def _():
        st_s[...] = jnp.zeros_like(st_s)

    for t in range(Lc):
        a = z_ref[:, pl.ds(t, NCH, stride=Lc), :].reshape(R, LW).T
        for g in range(GL):
            vt_s[g, Cn * t:Cn * (t + 1), :] = a[Cn * g:Cn * (g + 1), :]

    for g in range(GL):
        ub = vt_s[g].T.astype(BF16)
        z = jnp.dot(ub, w2_ref[g], preferred_element_type=F32)
        z1_s[...] = z[:, :P2]
        z2_s[...] = z[:, P2:]
        pv, qv, q2 = pq_ref[g, 0:1, :], pq_ref[g, 1:2, :], pq_ref[g, 2:3, :]
        x, xs = st_s[g, 0], st_s[g, 1]
        for c in range(NCH):
            rows = pl.ds(c, B, stride=NCH)
            xp_s[rows, :] = x
            x, xs = x * pv + xs * qv + z1_s[rows, :], xs * pv + x * q2 + z2_s[rows, :]
        st_s[g, 0] = x
        st_s[g, 1] = xs
        y = (jnp.dot(ub, t_ref[g], preferred_element_type=F32)
             + lax.dot_general(xp_s[...].astype(BF16), v_ref[g], (((1,), (1,)), ((), ())),
                               preferred_element_type=F32))
        yt = y.T
        for t in range(Lc):
            yt_s[t, Cn * g:Cn * (g + 1), :] = yt[Cn * t:Cn * (t + 1), :]

    for t in range(Lc):
        y_ref[:, pl.ds(t, NCH, stride=Lc), :] = yt_s[t].T.reshape(B, NCH, LW)


def _s5(z3, col0, w2, tmat, vt, pq, layer, Cn, Lc, tt=512):
    B, S, _ = z3.shape
    _, G, W, _ = w2.shape
    P2 = vt.shape[3]
    GL = V7X_LANES // Cn
    R = B * (tt // Lc)
    wspec = lambda a: pl.BlockSpec((None, GL) + a.shape[2:], lambda j, t: (layer, j, 0, 0))
    return pl.pallas_call(
        functools.partial(_s5_kernel, Lc=Lc, Cn=Cn),
        grid=(G // GL, S // tt),
        in_specs=[pl.BlockSpec((B, tt, V7X_LANES), lambda j, t: (0, t, col0 // V7X_LANES + j)),
                  wspec(w2), wspec(tmat), wspec(vt), wspec(pq)],
        out_specs=pl.BlockSpec((B, tt, V7X_LANES), lambda j, t: (0, t, j)),
        out_shape=jax.ShapeDtypeStruct((B, S, G * Cn), F32),
        scratch_shapes=[pltpu.VMEM((GL, W, R), F32), pltpu.VMEM((R, P2), F32), pltpu.VMEM((R, P2), F32),
                        pltpu.VMEM((R, P2), F32), pltpu.VMEM((Lc, V7X_LANES, R), F32),
                        pltpu.VMEM((GL, 2, B, P2), F32)],
        compiler_params=_cparams(("parallel", "arbitrary")),
        name="s5",
    )(z3, w2, tmat, vt, pq)


def _layer_norm(v, g, b):
    mu = jnp.mean(v, axis=-1, keepdims=True)
    vc = v - mu
    var = jnp.mean(vc * vc, axis=-1, keepdims=True)
    return vc * lax.rsqrt(var + EPS) * g + b


def _mixout_kernel(ym_ref, ys_ref, us_ref, dsk_ref, x_ref, wglu_ref, bglu_ref, s5g_ref, wom_ref, wos_ref,
                   g1_ref, b1_ref, wrh_ref, wrl_ref, br_ref, x1_ref, x1p_ref, ri_ref, rw_ref,
                   *, alpha, n_grp, epg):
    ys = ys_ref[...] + dsk_ref[...] * us_ref[...]
    gy = 0.5 * ys * (1.0 + jnp.tanh(math.sqrt(2.0 / math.pi) * (ys + 0.044715 * (ys * ys * ys))))
    glu = gy * jax.nn.sigmoid(jnp.dot(gy.astype(BF16), wglu_ref[...], preferred_element_type=F32)
                              + bglu_ref[...])
    ysn = glu * lax.rsqrt(jnp.mean(glu * glu, axis=-1, keepdims=True) + EPS) * s5g_ref[...]
    mix = (jnp.dot(ym_ref[...].astype(BF16), wom_ref[...], preferred_element_type=F32)
           + jnp.dot(ysn.astype(BF16), wos_ref[...], preferred_element_type=F32))
    x1 = _layer_norm(alpha * x_ref[...] + mix, g1_ref[...], b1_ref[...])
    x1_ref[...] = x1
    _store_token_rows(x1p_ref, _pack_bf16_pairs(x1))

    xh = x1.astype(BF16)
    xl = (x1 - xh.astype(F32)).astype(BF16)
    logits = (jnp.dot(xh, wrh_ref[...], preferred_element_type=F32)
              + jnp.dot(xl, wrh_ref[...], preferred_element_type=F32)
              + jnp.dot(xh, wrl_ref[...], preferred_element_type=F32) + br_ref[...])
    tm = logits.shape[0]
    lane = _iota((tm, V7X_LANES), 1)
    big = jnp.int32(V7X_LANES)
    neg = -jnp.inf
    gl = jnp.where(lane < n_grp, logits, neg)
    gmax = jnp.max(gl, axis=-1, keepdims=True)
    grp = jnp.min(jnp.where(gl == gmax, lane, big), axis=-1, keepdims=True)
    p_grp = 1.0 / jnp.sum(jnp.exp(gl - gmax), axis=-1, keepdims=True)
    lo = n_grp + grp * epg
    el = jnp.where((lane >= lo) & (lane < lo + epg), logits, neg)
    v1 = jnp.max(el, axis=-1, keepdims=True)
    i1 = jnp.min(jnp.where(el == v1, lane, big), axis=-1, keepdims=True)
    el2 = jnp.where(lane == i1, neg, el)
    v2 = jnp.max(el2, axis=-1, keepdims=True)
    i2 = jnp.min(jnp.where(el2 == v2, lane, big), axis=-1, keepdims=True)
    e2 = jnp.exp(v2 - v1)
    w1 = p_grp / (1.0 + e2)
    w2 = p_grp * e2 / (1.0 + e2)
    ri_ref[...] = jnp.where(lane == 0, i1 - n_grp, jnp.where(lane == 1, i2 - n_grp, 0))
    rw_ref[...] = jnp.where(lane == 0, w1, jnp.where(lane == 1, w2, 0.0))


def _mixout(ym, ys, z_main, us_col, dsk, x2d, wglu, bglu, s5g, wom, wos, g1, b1, wrh, wrl, br,
            alpha, n_grp, epg, tm=512):
    M, D = x2d.shape
    Dm, Ds = ym.shape[1], ys.shape[1]
    tile = lambda w: pl.BlockSpec((tm, w), lambda i: (i, 0))
    full = lambda a: pl.BlockSpec(a.shape, lambda i: (0,) * a.ndim)
    return pl.pallas_call(
        functools.partial(_mixout_kernel, alpha=alpha, n_grp=n_grp, epg=epg),
        grid=(M // tm,),
        in_specs=[tile(Dm), tile(Ds), pl.BlockSpec((tm, Ds), lambda i: (i, us_col // Ds)), full(dsk), tile(D),
                  full(wglu), full(bglu), full(s5g), full(wom), full(wos),
                  full(g1), full(b1), full(wrh), full(wrl), full(br)],
        out_specs=[tile(D), pl.BlockSpec((tm * TOKEN_ROWS, V7X_LANES), lambda i: (i, 0)),
                   tile(V7X_LANES), tile(V7X_LANES)],
        out_shape=[jax.ShapeDtypeStruct((M, D), F32),
                   jax.ShapeDtypeStruct((M * TOKEN_ROWS, V7X_LANES), U32),
                   jax.ShapeDtypeStruct((M, V7X_LANES), I32),
                   jax.ShapeDtypeStruct((M, V7X_LANES), F32)],
        compiler_params=_cparams(("parallel",)),
        name="mixout",
    )(ym, ys, z_main, dsk, x2d, wglu, bglu, s5g, wom, wos, g1, b1, wrh, wrl, br)


def _dispatch_kernel(ri_ref, dest_ref, blk_ref, run_s, pst_s, *, n_exp, rows, nbp):
    ph = pl.program_id(0)
    i = pl.program_id(1)
    tm = ri_ref.shape[0]
    lane = _iota((tm, V7X_LANES), 1)
    oh0 = lane == ri_ref[:, 0:1]
    oh1 = lane == ri_ref[:, 1:2]
    cnt = jnp.where(oh0 | oh1, 1.0, 0.0).astype(F32)
    tile_cnt = jnp.sum(cnt, axis=0, keepdims=True)

    @pl.when((ph == 0) & (i == 0))
    def _():
        run_s[...] = jnp.zeros_like(run_s)

    @pl.when(ph == 0)
    def _():
        run_s[...] = run_s[...] + tile_cnt
        dest_ref[...] = jnp.zeros_like(dest_ref)

    @pl.when((ph == 1) & (i == 0))
    def _():
        counts = run_s[...]
        nblk = jnp.floor((counts + (rows - 1)) / rows)
        tri = jnp.where(_iota((V7X_LANES, V7X_LANES), 0) < _iota((V7X_LANES, V7X_LANES), 1), 1.0, 0.0)
        bstart = jnp.dot(jnp.broadcast_to(nblk, (V7X_SUBLANES, V7X_LANES)), tri.astype(F32),
                         precision=HIGHEST, preferred_element_type=F32)[0:1, :]
        pst_s[...] = bstart * rows
        bend = bstart + nblk
        j = _iota((nbp, V7X_LANES), 0).astype(F32)
        ln = _iota((nbp, V7X_LANES), 1)
        done = jnp.sum(jnp.where((ln < n_exp) & (bend <= j), 1.0, 0.0), axis=1, keepdims=True)
        blk = jnp.minimum(done, n_exp - 1.0)
        used = jnp.sum(jnp.where(ln == n_exp - 1, bend, 0.0), axis=1, keepdims=True)
        mine = ln.astype(F32) == blk
        cnt_j = jnp.sum(jnp.where(mine, counts, 0.0), axis=1, keepdims=True)
        bst_j = jnp.sum(jnp.where(mine, bstart, 0.0), axis=1, keepdims=True)
        valid = jnp.clip(cnt_j - (j[:, 0:1] - bst_j) * rows, 0.0, float(rows))
        blk_ref[...] = jnp.where(ln == 0, blk, jnp.where(ln == 1, used, jnp.where(ln == 2, valid, 0.0))
                                 ).astype(I32)
        run_s[...] = jnp.zeros_like(run_s)

    @pl.when(ph == 1)
    def _():
        lower = jnp.where(_iota((tm, tm), 0) > _iota((tm, tm), 1), 1.0, 0.0).astype(BF16)
        excl = jnp.dot(lower, cnt.astype(BF16), preferred_element_type=F32)
        base = excl + run_s[...] + pst_s[...]
        d0 = jnp.sum(jnp.where(oh0, base, 0.0), axis=1, keepdims=True)
        d1 = jnp.sum(jnp.where(oh1, base, 0.0), axis=1, keepdims=True)
        dest_ref[...] = jnp.where(lane == 0, d0, jnp.where(lane == 1, d1, 0.0)).astype(I32)
        run_s[...] = run_s[...] + tile_cnt


def _dispatch(route_i, n_exp, rows, nbp, tm=512):
    M = route_i.shape[0]
    return pl.pallas_call(
        functools.partial(_dispatch_kernel, n_exp=n_exp, rows=rows, nbp=nbp),
        grid=(2, M // tm),
        in_specs=[pl.BlockSpec((tm, V7X_LANES), lambda p, i: (i, 0))],
        out_specs=[pl.BlockSpec((tm, V7X_LANES), lambda p, i: (i * p, 0)),
                   pl.BlockSpec((nbp, V7X_LANES), lambda p, i: (0, 0))],
        out_shape=[jax.ShapeDtypeStruct((M, V7X_LANES), I32),
                   jax.ShapeDtypeStruct((nbp, V7X_LANES), I32)],
        scratch_shapes=[pltpu.VMEM((1, V7X_LANES), F32), pltpu.VMEM((1, V7X_LANES), F32)],
        compiler_params=_cparams(("arbitrary", "arbitrary")),
        name="dispatch",
    )(route_i)


def _row_copy(src, dst, sem, r_src, r_dst):
    rows = lambda r: pl.ds(pl.multiple_of(r * TOKEN_ROWS, TOKEN_ROWS), TOKEN_ROWS)
    return pltpu.make_async_copy(src.at[rows(r_src), :], dst.at[rows(r_dst), :], sem)


def _scatter_kernel(dest_ref, nvalid_ref, x_ref, xs_out, zbuf, sem, zsem, *, rows):
    tm = x_ref.shape[0] // TOKEN_ROWS
    blk_rows = rows * TOKEN_ROWS
    base = pl.program_id(0) * (tm * TOP_K)

    @pl.when(pl.program_id(0) == 0)
    def _():
        zbuf[...] = jnp.zeros_like(zbuf)
        nblocks = xs_out.shape[0] // blk_rows
        zero_copy = lambda j: pltpu.make_async_copy(zbuf, xs_out.at[pl.ds(j * blk_rows, blk_rows), :], zsem)

        def start(j, _):
            @pl.when(nvalid_ref[j] < rows)
            def _():
                zero_copy(j).start()
            return 0

        def wait(j, _):
            @pl.when(nvalid_ref[j] < rows)
            def _():
                zero_copy(j).wait()
            return 0

        lax.fori_loop(0, nblocks, start, 0)
        lax.fori_loop(0, nblocks, wait, 0)

    def issue(r, _):
        for k in range(TOP_K):
            _row_copy(x_ref, xs_out, sem, r, dest_ref[base + r * TOP_K + k]).start()
        return 0

    lax.fori_loop(0, tm, issue, 0, unroll=8)

    def drain(r, _):
        for k in range(TOP_K):
            _row_copy(x_ref, xs_out, sem, 0, 0).wait()
        return 0

    lax.fori_loop(0, tm, drain, 0, unroll=8)


def _scatter(dest, nvalid, x1p, rows, tm=256):
    M = x1p.shape[0] // TOKEN_ROWS
    return pl.pallas_call(
        functools.partial(_scatter_kernel, rows=rows),
        grid_spec=pltpu.PrefetchScalarGridSpec(
            num_scalar_prefetch=2,
            grid=(M // tm,),
            in_specs=[pl.BlockSpec((tm * TOKEN_ROWS, V7X_LANES), lambda i, d, n: (i, 0))],
            out_specs=pl.BlockSpec(memory_space=pl.ANY),
            scratch_shapes=[pltpu.VMEM((rows * TOKEN_ROWS, V7X_LANES), x1p.dtype),
                            pltpu.SemaphoreType.DMA(()), pltpu.SemaphoreType.DMA(())]),
        out_shape=jax.ShapeDtypeStruct((nvalid.shape[0] * rows * TOKEN_ROWS, V7X_LANES), x1p.dtype),
        compiler_params=_cparams(("arbitrary",)),
        name="scatter",
    )(dest, nvalid, x1p)


def _experts_kernel(blk_ref, used_ref, xs_ref, wg_hbm, wu_hbm, wd_hbm, o_ref,
                    fg, fu, fd, wg_s, wu_s, wd_s, slot_s, wsem, *, layer):
    j = pl.program_id(0)
    used = used_ref[0]
    nblk = pl.num_programs(0)
    active = j < used
    e = blk_ref[j]

    def fetch(expert, slot):
        return [pltpu.make_async_copy(w.at[layer, expert], f.at[slot], wsem.at[slot, t])
                for t, (w, f) in enumerate(((wg_hbm, fg), (wu_hbm, fu), (wd_hbm, fd)))]

    @pl.when(active & (j == 0))
    def _():
        slot_s[0] = 0
        for c in fetch(e, 0):
            c.start()

    @pl.when(active & ((j == 0) | (e != blk_ref[jnp.maximum(j - 1, 0)])))
    def _():
        slot = slot_s[0]
        for c in fetch(e, slot):
            c.wait()
        wg_s[...] = fg[slot].astype(BF16)
        wu_s[...] = fu[slot].astype(BF16)
        wd_s[...] = fd[slot].astype(BF16)
        nxt = lax.while_loop(lambda t: (t < used) & (blk_ref[jnp.minimum(t, nblk - 1)] == e), lambda t: t + 1, j + 1)

        @pl.when(nxt < used)
        def _():
            for c in fetch(blk_ref[jnp.minimum(nxt, nblk - 1)], 1 - slot):
                c.start()

        slot_s[0] = 1 - slot

    @pl.when(active)
    def _():
        lo, hi = _unpack_bf16_pairs(_load_token_rows(xs_ref))
        xb = jnp.concatenate([lo.astype(BF16), hi.astype(BF16)], axis=1)
        g = jnp.dot(xb, wg_s[...], preferred_element_type=F32)
        u = jnp.dot(xb, wu_s[...], preferred_element_type=F32)
        hmid = (g * jax.nn.sigmoid(g) * u).astype(BF16)
        _store_token_rows(o_ref, _pack_bf16_pairs(jnp.dot(hmid, wd_s[...], preferred_element_type=F32)))

    @pl.when(j >= used)
    def _():
        o_ref[...] = jnp.zeros_like(o_ref)


def _experts(blk_e, used, xs, wg, wu, wd, layer, rows):
    P = xs.shape[0] // TOKEN_ROWS
    tile = pl.BlockSpec((rows * TOKEN_ROWS, V7X_LANES), lambda j, b, u: (j, 0))
    D, De = wg.shape[2], wg.shape[3]
    nb = P // rows
    hbm = pl.BlockSpec(memory_space=pl.ANY)
    return pl.pallas_call(
        functools.partial(_experts_kernel, layer=layer),
        grid_spec=pltpu.PrefetchScalarGridSpec(
            num_scalar_prefetch=2,
            grid=(nb,),
            in_specs=[tile, hbm, hbm, hbm],
            out_specs=tile,
            scratch_shapes=[pltpu.VMEM((2, D, De), F32), pltpu.VMEM((2, D, De), F32), pltpu.VMEM((2, De, D), F32),
                            pltpu.VMEM((D, De), BF16), pltpu.VMEM((D, De), BF16), pltpu.VMEM((De, D), BF16),
                            pltpu.SMEM((1,), I32), pltpu.SemaphoreType.DMA((2, 3))]),
        out_shape=jax.ShapeDtypeStruct((P * TOKEN_ROWS, V7X_LANES), U32),
        compiler_params=_cparams(("arbitrary",)),
        name="experts",
    )(blk_e, used, xs, wg, wu, wd)


def _post_kernel(dest_ref, x1_ref, rw_ref, p_ref, os_hbm, g2_ref, b2_ref, wpg_ref, bpg_ref, wpp_ref, pg_ref,
                 out_ref, gbuf, sem, *, alpha):
    i = pl.program_id(0)
    last = pl.num_programs(0) - 1
    tm = x1_ref.shape[0]
    slot = lax.rem(i, 2)

    def gather_row(tile, slot_, r):
        for k in range(TOP_K):
            _row_copy(os_hbm, gbuf.at[slot_, k], sem.at[slot_], dest_ref[tile * (tm * TOP_K) + r * TOP_K + k],
                      r).start()

    def drain(slot_):
        def body(r, _):
            for k in range(TOP_K):
                _row_copy(os_hbm, gbuf.at[slot_, k], sem.at[slot_], 0, 0).wait()
            return 0

        lax.fori_loop(0, tm, body, 0, unroll=8)

    @pl.when(i == 0)
    def _():
        def first(r, _):
            gather_row(0, 0, r)
            return 0

        lax.fori_loop(0, tm, first, 0, unroll=8)

    drain(slot)
    nxt = jnp.minimum(i + 1, last)
    for r in range(tm):
        gather_row(nxt, 1 - slot, r)

    rw = rw_ref[...]
    lo0, hi0 = _unpack_bf16_pairs(_load_token_rows(gbuf.at[slot, 0]))
    lo1, hi1 = _unpack_bf16_pairs(_load_token_rows(gbuf.at[slot, 1]))
    w0, w1 = rw[:, 0:1], rw[:, 1:2]
    ffn = jnp.concatenate([w0 * lo0 + w1 * lo1, w0 * hi0 + w1 * hi1], axis=1)
    x2 = _layer_norm(alpha * x1_ref[...] + ffn, g2_ref[...], b2_ref[...])
    gate = jax.nn.sigmoid(jnp.dot(x2.astype(BF16), wpg_ref[...], preferred_element_type=F32) + bpg_ref[...])
    pp = jnp.dot(p_ref[...].astype(BF16), wpp_ref[...], preferred_element_type=F32)
    ple = pp * lax.rsqrt(jnp.mean(pp * pp, axis=-1, keepdims=True) + EPS) * pg_ref[...]
    out_ref[...] = x2 + gate * ple

    @pl.when(i == last)
    def _():
        drain(1 - slot)


def _post(dest, x1, route_w, p3d, layer, os, g2, b2, wpg, bpg, wpp, pg, alpha, tm=256):
    M, D = x1.shape
    Dp = p3d.shape[2]
    tile = lambda w: pl.BlockSpec((tm, w), lambda i, d: (i, 0))
    full = lambda a: pl.BlockSpec(a.shape, lambda i, d: (0,) * a.ndim)
    return pl.pallas_call(
        functools.partial(_post_kernel, alpha=alpha),
        grid_spec=pltpu.PrefetchScalarGridSpec(
            num_scalar_prefetch=1,
            grid=(M // tm,),
            in_specs=[tile(D), tile(V7X_LANES), pl.BlockSpec((None, tm, Dp), lambda i, d: (layer, i, 0)),
                      pl.BlockSpec(memory_space=pl.ANY),
                      full(g2), full(b2), full(wpg), full(bpg), full(wpp), full(pg)],
            out_specs=tile(D),
            scratch_shapes=[pltpu.VMEM((2, TOP_K, tm * TOKEN_ROWS, V7X_LANES), U32), pltpu.SemaphoreType.DMA((2,))]),
        out_shape=jax.ShapeDtypeStruct((M, D), F32),
        compiler_params=_cparams(("arbitrary",)),
        name="post",
    )(dest, x1, route_w, p3d, os, g2, b2, wpg, bpg, wpp, pg)


def kernel(x, p, w_in, conv_w, conv_b, w_q, w_k, b_i, b_f, mh_g, lam_re, lam_im, log_dt, b_re, b_im, c_re, c_im, d_skip, w_glu, b_glu, s5_g, w_out, ln1_g, ln1_b, w_grp, b_grp, w_rt, b_rt, w_eg, w_eu, w_ed, ln2_g, ln2_b, w_pg, b_pg, w_pp, ple_g):
    B, S, D = x.shape
    depth = w_in.shape[0]
    H, Dh = w_q.shape[1], w_q.shape[2]
    Dm = H * Dh
    G, Cn = d_skip.shape[1], d_skip.shape[2]
    Ds = G * Cn
    n_grp = w_grp.shape[-1]
    n_exp = w_eg.shape[1]
    epg = n_exp // n_grp
    M = B * S
    A = M * TOP_K
    alpha = (2 * depth) ** 0.25
    NC = S // MLSTM_CHUNK
    nblocks = A // MOE_ROWS + n_exp
    nbp = -(-nblocks // V7X_SUBLANES) * V7X_SUBLANES
    row2 = lambda a: a.reshape(1, -1).astype(F32)

    s5_all = _s5_operators(lam_re, lam_im, log_dt, b_re, b_im, c_re, c_im, S5_CHUNK)

    xc = x.reshape(M, D).astype(F32)
    for l in range(depth):
        wi = w_in[l]
        w_main = jnp.concatenate([wi[:, :3 * Dm], wi[:, 3 * Dm + 2 * H:]], axis=1).astype(BF16)
        w_gate = jnp.pad(wi[:, 3 * Dm:3 * Dm + 2 * H], ((0, 0), (0, V7X_LANES - 2 * H))).astype(BF16)
        w_router = jnp.pad(jnp.concatenate([w_grp[l], w_rt[l]], axis=1).astype(F32),
                           ((0, 0), (0, V7X_LANES - n_grp - n_exp)))
        w_router_hi = w_router.astype(BF16)
        w_router_lo = (w_router - w_router_hi.astype(F32)).astype(BF16)
        b_router = jnp.pad(jnp.concatenate([b_grp[l], b_rt[l]]).astype(F32),
                           (0, V7X_LANES - n_grp - n_exp)).reshape(1, -1)

        z_main, z_gate = _inproj(xc, w_main, w_gate)

        gates = z_gate[:, :2 * H].reshape(B, NC, MLSTM_CHUNK, 2, H).transpose(0, 4, 3, 1, 2)
        y_m = _mlstm(z_main.reshape(B, S, -1), gates, jnp.stack([b_i[l], b_f[l]]).astype(F32),
                     conv_w[l].astype(F32), row2(conv_b[l]), w_q[l].astype(BF16), w_k[l].astype(BF16),
                     row2(mh_g[l]), H, Dh).reshape(M, Dm)

        y_s = _s5(z_main.reshape(B, S, -1), 3 * Dm, *s5_all, l, Cn, S5_CHUNK).reshape(M, Ds)

        wo = w_out[l].astype(BF16)
        x1, x1p, route_i, route_w = _mixout(y_m, y_s, z_main, 3 * Dm, row2(d_skip[l]), xc,
                                       w_glu[l].astype(BF16), row2(b_glu[l]), row2(s5_g[l]),
                                       wo[:Dm], wo[Dm:], row2(ln1_g[l]), row2(ln1_b[l]),
                                       w_router_hi, w_router_lo, b_router, alpha, n_grp, epg)

        dest_l, blk_l = _dispatch(route_i, n_exp, MOE_ROWS, nbp)
        dest = dest_l[:, :TOP_K].reshape(A)
        xs = _scatter(dest, blk_l[:nblocks, 2], x1p, MOE_ROWS)
        os = _experts(blk_l[:nblocks, 0], blk_l[:1, 1], xs, w_eg, w_eu, w_ed, l, MOE_ROWS)
        xc = _post(dest, x1, route_w, p.reshape(depth, M, -1), l, os, row2(ln2_g[l]), row2(ln2_b[l]),
                   w_pg[l].astype(BF16), row2(b_pg[l]), w_pp[l].astype(BF16), row2(ple_g[l]), alpha)
    return xc.reshape(B, S, D).astype(x.dtype)
```

```python
import functools
import math

import jax
import jax.numpy as jnp
from jax import lax
from jax.experimental import pallas as pl
from jax.experimental.pallas import tpu as pltpu

F32 = jnp.float32
BF16 = jnp.bfloat16
I32 = jnp.int32
HIGHEST = lax.Precision.HIGHEST

EPS = 1e-5
V7X_LANES = 128
V7X_SUBLANES = 8
VMEM_LIMIT = 56 * 1024 * 1024

MLSTM_CHUNK = 256
S5_CHUNK = 16
MOE_ROWS = 512
TOP_K = 2


def _cparams(sem):
    return pltpu.CompilerParams(dimension_semantics=sem, vmem_limit_bytes=VMEM_LIMIT)


def _iota(shape, axis):
    return lax.broadcasted_iota(I32, shape, axis)


U32 = jnp.uint32
_HI16 = 0xFFFF0000


def _pack_bf16_pairs(v):
    W = v.shape[1] // 2
    bits = lax.bitcast_convert_type(v.astype(BF16).astype(F32), U32)
    return (bits[:, :W] >> 16) | (bits[:, W:] & U32(_HI16))


def _unpack_bf16_pairs(w):
    return lax.bitcast_convert_type(w << 16, F32), lax.bitcast_convert_type(w & U32(_HI16), F32)


TOKEN_ROWS = 4


def _store_token_rows(ref, words):
    R = words.shape[0]
    for q in range(TOKEN_ROWS):
        ref[pl.ds(q, R, stride=TOKEN_ROWS), :] = words[:, q * V7X_LANES:(q + 1) * V7X_LANES]


def _load_token_rows(ref):
    R = ref.shape[0] // TOKEN_ROWS
    return jnp.concatenate([ref[pl.ds(q, R, stride=TOKEN_ROWS), :] for q in range(TOKEN_ROWS)], axis=1)


def _inproj_kernel(x_ref, wm_ref, wg_ref, zm_ref, zg_ref):
    xb = x_ref[...].astype(BF16)
    zm_ref[...] = jnp.dot(xb, wm_ref[...], preferred_element_type=F32)
    zg_ref[...] = jnp.dot(xb, wg_ref[...], preferred_element_type=F32)


def _inproj(x2d, w_main, w_gate, tm=512):
    M, D = x2d.shape
    N = w_main.shape[1]
    return pl.pallas_call(
        _inproj_kernel,
        grid=(M // tm,),
        in_specs=[pl.BlockSpec((tm, D), lambda i: (i, 0)),
                  pl.BlockSpec((D, N), lambda i: (0, 0)),
                  pl.BlockSpec((D, V7X_LANES), lambda i: (0, 0))],
        out_specs=[pl.BlockSpec((tm, N), lambda i: (i, 0)),
                   pl.BlockSpec((tm, V7X_LANES), lambda i: (i, 0))],
        out_shape=[jax.ShapeDtypeStruct((M, N), F32),
                   jax.ShapeDtypeStruct((M, V7X_LANES), F32)],
        compiler_params=_cparams(("parallel",)),
        name="inproj",
    )(x2d, w_main, w_gate)


def _mlstm_kernel(bif_ref, um_ref, v_ref, o_ref, gt_ref, cw_ref, cb_ref, wq_ref, wk_ref, mhg_ref,
                  y_ref, q_s, k_s, r_s, so_s, sn_s, parts_s, st_s):
    h = pl.program_id(1)
    S, Dh = um_ref.shape
    NC, L = gt_ref.shape[1], gt_ref.shape[2]
    NQ = 4

    u = um_ref[...]
    row = _iota((S, Dh), 0)
    width = cw_ref.shape[0]
    acc = cb_ref[...] + cw_ref[width - 1:width, :] * u
    for j in range(width - 1):
        sh = width - 1 - j
        us = jnp.where(row >= sh, pltpu.roll(u, sh, axis=0), 0.0)
        acc = acc + cw_ref[j:j + 1, :] * us
    cb = (acc * jax.nn.sigmoid(acc)).astype(BF16)
    q_s[...] = jnp.dot(cb, wq_ref[...], preferred_element_type=F32).astype(BF16)
    k_s[...] = jnp.dot(cb, wk_ref[...], preferred_element_type=F32) * (Dh ** -0.5)

    ig = gt_ref[0] + bif_ref[0, h]
    fp = gt_ref[1] + bif_ref[1, h]
    lf = jnp.minimum(fp, 0.0) - jnp.log1p(jnp.exp(-jnp.abs(fp)))
    tri = jnp.where(_iota((L, L), 0) <= _iota((L, L), 1), 1.0, 0.0).astype(F32)
    bcum = jnp.dot(lf, tri, precision=HIGHEST, preferred_element_type=F32)
    g = bcum[:, L - 1:L]
    a = g - bcum + ig
    m_loc = jnp.max(a, axis=-1, keepdims=True)
    r = ig - bcum
    r_s[...] = r
    lane = _iota((NC, L), 1)
    rcm = r
    sh = 1
    while sh < L:
        rcm = jnp.maximum(rcm, jnp.where(lane >= sh, pltpu.roll(rcm, sh, axis=1), -jnp.inf))
        sh *= 2
    m = jnp.zeros((1, 1), F32)
    m_prev, s_old, s_new = [], [], []
    for c in range(NC):
        m_prev.append(m)
        m_new = jnp.maximum(g[c:c + 1] + m, m_loc[c:c + 1])
        s_old.append(jnp.exp(g[c:c + 1] + m - m_new))
        s_new.append(jnp.exp(m_loc[c:c + 1] - m_new))
        m = m_new
    m_prev = jnp.concatenate(m_prev, axis=0)
    so_s[...] = jnp.broadcast_to(jnp.concatenate(s_old, axis=0), (NC, 2 * Dh))
    sn_s[...] = jnp.broadcast_to(jnp.concatenate(s_new, axis=0), (NC, 2 * Dh))
    mx = jnp.maximum(m_prev, rcm)
    m_t = bcum + mx
    per_token = (-mx,
                 jnp.exp(m_prev - mx),
                 jnp.exp(-m_t),
                 jnp.exp(a - m_loc))

    kk = 4 * NQ
    er_, ec_ = _iota((kk, NQ * Dh), 0), _iota((kk, NQ * Dh), 1)
    sel = jnp.where((er_ < 3 * NQ) & (er_ % NQ == ec_ // Dh), 1.0, 0.0).astype(BF16)
    for c in range(NC):
        x4 = jnp.concatenate([p[c:c + 1] for p in per_token], axis=0)
        hi = x4.astype(BF16).astype(F32)
        mid = (x4 - hi).astype(BF16).astype(F32)
        lo = (x4 - hi - mid).astype(BF16).astype(F32)
        parts_s[c] = jnp.concatenate([hi, mid, lo, jnp.zeros_like(hi)], axis=0).astype(BF16)

    st_s[...] = jnp.zeros_like(st_s)
    causal = _iota((L, L), 0) >= _iota((L, L), 1)
    ones_b = jnp.ones((L, Dh), BF16)

    def chunk(c, _):
        t0 = pl.multiple_of(c * L, L)
        cols = lax.dot_general(parts_s[c], sel, (((0,), (0,)), ((), ())), preferred_element_type=F32)
        bm, inter, emt, ean = (cols[:, q * Dh:(q + 1) * Dh] for q in range(NQ))
        qb = q_s[pl.ds(t0, L), :]
        kc = k_s[pl.ds(t0, L), :]
        v1 = jnp.concatenate([v_ref[pl.ds(t0, L), :].astype(BF16), ones_b], axis=1)

        log_dm = jnp.where(causal, jnp.concatenate([bm] * (L // Dh), axis=1) + r_s[pl.ds(c, 1), :], -jnp.inf)
        s = lax.dot_general(qb, kc.astype(BF16), (((1,), (1,)), ((), ())),
                            preferred_element_type=F32) * jnp.exp(log_dm)
        state = st_s[...]
        nd = (jnp.dot(s.astype(BF16), v1, preferred_element_type=F32)
              + jnp.concatenate([inter, inter], axis=1)
              * jnp.dot(qb, state.astype(BF16), preferred_element_type=F32))
        den = jnp.maximum(jnp.abs(nd[:, Dh:]), emt)
        hh = jax.nn.sigmoid(o_ref[pl.ds(t0, L), :]) * (nd[:, :Dh] / den)
        mu = jnp.mean(hh, axis=-1, keepdims=True)
        hc = hh - mu
        var = jnp.mean(hc * hc, axis=-1, keepdims=True)
        y_ref[pl.ds(t0, L), :] = hc * lax.rsqrt(var + EPS) * mhg_ref[...]

        upd = lax.dot_general((ean * kc).astype(BF16), v1, (((0,), (0,)), ((), ())), preferred_element_type=F32)
        st_s[...] = so_s[pl.ds(c, 1), :] * state + sn_s[pl.ds(c, 1), :] * upd
        return 0

    lax.fori_loop(0, NC, chunk, 0, unroll=4)


def _mlstm(z3, gates, b_if, conv_w, conv_b, wq, wk, mh_g, H, Dh):
    B, S, _ = z3.shape
    NC, L = gates.shape[3], gates.shape[4]
    blk = lambda off: pl.BlockSpec((None, S, Dh), lambda b, h: (b, 0, off + h))
    return pl.pallas_call(
        _mlstm_kernel,
        grid=(B, H),
        in_specs=[pl.BlockSpec(memory_space=pltpu.SMEM),
                  blk(0), blk(H), blk(2 * H),
                  pl.BlockSpec((None, None, 2, NC, L), lambda b, h: (b, h, 0, 0, 0)),
                  pl.BlockSpec((conv_w.shape[0], Dh), lambda b, h: (0, h)),
                  pl.BlockSpec((1, Dh), lambda b, h: (0, h)),
                  pl.BlockSpec((None, Dh, Dh), lambda b, h: (h, 0, 0)),
                  pl.BlockSpec((None, Dh, Dh), lambda b, h: (h, 0, 0)),
                  pl.BlockSpec((1, Dh), lambda b, h: (0, h))],
        out_specs=pl.BlockSpec((None, S, Dh), lambda b, h: (b, 0, h)),
        out_shape=jax.ShapeDtypeStruct((B, S, H * Dh), F32),
        scratch_shapes=[pltpu.VMEM((S, Dh), BF16), pltpu.VMEM((S, Dh), F32), pltpu.VMEM((NC, L), F32),
                        pltpu.VMEM((NC, 2 * Dh), F32), pltpu.VMEM((NC, 2 * Dh), F32),
                        pltpu.VMEM((NC, 16, L), BF16), pltpu.VMEM((Dh, 2 * Dh), F32)],
        compiler_params=_cparams(("parallel", "parallel")),
        name="mlstm",
    )(b_if, z3, z3, z3, gates, conv_w, conv_b, wq, wk, mh_g)


def _s5_operators_kernel(lr_ref, li_ref, ldt_ref, b1_ref, b2_ref, b3_ref, b4_ref, c1_ref, c2_ref,
                         w2_ref, t_ref, vt_ref, pq_ref, *, Lc):
    GB, Cn, P2 = b1_ref.shape
    W = Lc * Cn
    for g in range(GB):
        lr, li = lr_ref[g], li_ref[g]
        dt = jnp.exp(ldt_ref[g])
        er = jnp.exp(lr * dt)
        ar, ai = er * jnp.cos(li * dt), er * jnp.sin(li * dt)
        mag2 = lr * lr + li * li
        xr, xi = ar - 1.0, ai
        cr = (xr * lr + xi * li) / mag2
        ci = (xi * lr - xr * li) / mag2
        bb1 = cr * b1_ref[g] + ci * b2_ref[g]
        bb2 = cr * b2_ref[g] - ci * b1_ref[g]
        bb3 = cr * b3_ref[g] + ci * b4_ref[g]
        bb4 = cr * b4_ref[g] - ci * b3_ref[g]
        c1, c2 = c1_ref[g], c2_ref[g]
        pr, pi = [jnp.ones_like(ar)], [jnp.zeros_like(ai)]
        for _ in range(Lc):
            pr.append(pr[-1] * ar - pi[-1] * ai)
            pi.append(pr[-2] * ai + pi[-1] * ar)
        ca = jnp.concatenate([pr[k] * c1 + pi[k] * c2 for k in range(Lc + 1)], axis=0)
        vt_ref[g] = ca[Cn:, :].astype(BF16)
        wa = jnp.concatenate([pr[Lc - 1 - t] * bb1 + pi[Lc - 1 - t] * bb2 for t in range(Lc)], axis=0)
        wb = jnp.concatenate([pr[Lc - 1 - t] * bb3 + pi[Lc - 1 - t] * bb4 for t in range(Lc)], axis=0)
        w2_ref[g] = jnp.concatenate([wa, wb], axis=1).astype(BF16)
        kern = lax.dot_general(bb1, ca[:W, :], (((1,), (1,)), ((), ())), precision=HIGHEST,
                               preferred_element_type=F32)
        rows = [kern] + [jnp.concatenate([jnp.zeros((Cn, Cn * sft), F32), kern[:, :W - Cn * sft]], axis=1)
                         for sft in range(1, Lc)]
        t_ref[g] = jnp.concatenate(rows, axis=0).astype(BF16)
        sgn = jnp.where(_iota((1, P2), 1) < P2 // 2, -1.0, 1.0)
        q1 = pi[Lc] * sgn
        pq_ref[g] = jnp.concatenate([pr[Lc], q1, -q1, jnp.zeros((V7X_SUBLANES - 3, P2), F32)], axis=0)


def _s5_operators(lam_re, lam_im, log_dt, b_re, b_im, c_re, c_im, Lc, gb=8):
    Dd, G, P = lam_re.shape
    Cn = b_re.shape[-1]
    W = Lc * Cn
    f = lambda a: a.astype(F32)
    dup = lambda a: jnp.concatenate([f(a), f(a)], axis=-1)[:, :, None, :]
    brt, bit = f(b_re).swapaxes(-1, -2), f(b_im).swapaxes(-1, -2)
    cat = lambda x, y: jnp.concatenate([x, y], axis=-1)
    args = (dup(lam_re), dup(lam_im), dup(jnp.broadcast_to(log_dt[..., None], lam_re.shape)),
            cat(brt, bit), cat(-bit, brt), cat(bit, brt), cat(brt, -bit),
            cat(f(c_re), -f(c_im)), cat(-f(c_im), -f(c_re)))
    spec = lambda r, c: pl.BlockSpec((None, gb, r, c), lambda d, j: (d, j, 0, 0))
    return pl.pallas_call(
        functools.partial(_s5_operators_kernel, Lc=Lc),
        grid=(Dd, G // gb),
        in_specs=[spec(1, 2 * P)] * 3 + [spec(Cn, 2 * P)] * 6,
        out_specs=[spec(W, 4 * P), spec(W, W), spec(W, 2 * P), spec(V7X_SUBLANES, 2 * P)],
        out_shape=[jax.ShapeDtypeStruct((Dd, G, W, 4 * P), BF16), jax.ShapeDtypeStruct((Dd, G, W, W), BF16),
                   jax.ShapeDtypeStruct((Dd, G, W, 2 * P), BF16),
                   jax.ShapeDtypeStruct((Dd, G, V7X_SUBLANES, 2 * P), F32)],
        compiler_params=_cparams(("parallel", "parallel")),
        name="s5_operators",
    )(*args)


def _s5_kernel(z_ref, w2_ref, t_ref, v_ref, pq_ref, y_ref, vt_s, z1_s, z2_s, xp_s, yt_s, st_s, *, Lc, Cn):
    B, TT, LW = z_ref.shape
    GL = LW // Cn
    NCH = TT // Lc
    R = B * NCH
    P2 = xp_s.shape[1]

    @pl.when(pl.program_id(1) == 0)
    def _():
        st_s[...] = jnp.zeros_like(st_s)

    for t in range(Lc):
        a = z_ref[:, pl.ds(t, NCH, stride=Lc), :].reshape(R, LW).T
        for g in range(GL):
            vt_s[g, Cn * t:Cn * (t + 1), :] = a[Cn * g:Cn * (g + 1), :]

    for g in range(GL):
        ub = vt_s[g].T.astype(BF16)
        z = jnp.dot(ub, w2_ref[g], preferred_element_type=F32)
        z1_s[...] = z[:, :P2]
        z2_s[...] = z[:, P2:]
        pv, qv, q2 = pq_ref[g, 0:1, :], pq_ref[g, 1:2, :], pq_ref[g, 2:3, :]
        x, xs = st_s[g, 0], st_s[g, 1]
        for c in range(NCH):
            rows = pl.ds(c, B, stride=NCH)
            xp_s[rows, :] = x
            x, xs = x * pv + xs * qv + z1_s[rows, :], xs * pv + x * q2 + z2_s[rows, :]
        st_s[g, 0] = x
        st_s[g, 1] = xs
        y = (jnp.dot(ub, t_ref[g], preferred_element_type=F32)
             + lax.dot_general(xp_s[...].astype(BF16), v_ref[g], (((1,), (1,)), ((), ())),
                               preferred_element_type=F32))
        yt = y.T
        for t in range(Lc):
            yt_s[t, Cn * g:Cn * (g + 1), :] = yt[Cn * t:Cn * (t + 1), :]

    for t in range(Lc):
        y_ref[:, pl.ds(t, NCH, stride=Lc), :] = yt_s[t].T.reshape(B, NCH, LW)


def _s5(z3, col0, w2, tmat, vt, pq, layer, Cn, Lc, tt=512):
    B, S, _ = z3.shape
    _, G, W, _ = w2.shape
    P2 = vt.shape[3]
    GL = V7X_LANES // Cn
    R = B * (tt // Lc)
    wspec = lambda a: pl.BlockSpec((None, GL) + a.shape[2:], lambda j, t: (layer, j, 0, 0))
    return pl.pallas_call(
        functools.partial(_s5_kernel, Lc=Lc, Cn=Cn),
        grid=(G // GL, S // tt),
        in_specs=[pl.BlockSpec((B, tt, V7X_LANES), lambda j, t: (0, t, col0 // V7X_LANES + j)),
                  wspec(w2), wspec(tmat), wspec(vt), wspec(pq)],
        out_specs=pl.BlockSpec((B, tt, V7X_LANES), lambda j, t: (0, t, j)),
        out_shape=jax.ShapeDtypeStruct((B, S, G * Cn), F32),
        scratch_shapes=[pltpu.VMEM((GL, W, R), F32), pltpu.VMEM((R, P2), F32), pltpu.VMEM((R, P2), F32),
                        pltpu.VMEM((R, P2), F32), pltpu.VMEM((Lc, V7X_LANES, R), F32),
                        pltpu.VMEM((GL, 2, B, P2), F32)],
        compiler_params=_cparams(("parallel", "arbitrary")),
        name="s5",
    )(z3, w2, tmat, vt, pq)


def _layer_norm(v, g, b):
    mu = jnp.mean(v, axis=-1, keepdims=True)
    vc = v - mu
    var = jnp.mean(vc * vc, axis=-1, keepdims=True)
    return vc * lax.rsqrt(var + EPS) * g + b


def _mixout_kernel(ym_ref, ys_ref, us_ref, dsk_ref, x_ref, wglu_ref, bglu_ref, s5g_ref, wom_ref, wos_ref,
                   g1_ref, b1_ref, wrh_ref, wrl_ref, br_ref, x1_ref, x1p_ref, ri_ref, rw_ref,
                   *, alpha, n_grp, epg):
    ys = ys_ref[...] + dsk_ref[...] * us_ref[...]
    gy = 0.5 * ys * (1.0 + jnp.tanh(math.sqrt(2.0 / math.pi) * (ys + 0.044715 * (ys * ys * ys))))
    glu = gy * jax.nn.sigmoid(jnp.dot(gy.astype(BF16), wglu_ref[...], preferred_element_type=F32)
                              + bglu_ref[...])
    ysn = glu * lax.rsqrt(jnp.mean(glu * glu, axis=-1, keepdims=True) + EPS) * s5g_ref[...]
    mix = (jnp.dot(ym_ref[...].astype(BF16), wom_ref[...], preferred_element_type=F32)
           + jnp.dot(ysn.astype(BF16), wos_ref[...], preferred_element_type=F32))
    x1 = _layer_norm(alpha * x_ref[...] + mix, g1_ref[...], b1_ref[...])
    x1_ref[...] = x1
    _store_token_rows(x1p_ref, _pack_bf16_pairs(x1))

    xh = x1.astype(BF16)
    xl = (x1 - xh.astype(F32)).astype(BF16)
    logits = (jnp.dot(xh, wrh_ref[...], preferred_element_type=F32)
              + jnp.dot(xl, wrh_ref[...], preferred_element_type=F32)
              + jnp.dot(xh, wrl_ref[...], preferred_element_type=F32) + br_ref[...])
    tm = logits.shape[0]
    lane = _iota((tm, V7X_LANES), 1)
    big = jnp.int32(V7X_LANES)
    neg = -jnp.inf
    gl = jnp.where(lane < n_grp, logits, neg)
    gmax = jnp.max(gl, axis=-1, keepdims=True)
    grp = jnp.min(jnp.where(gl == gmax, lane, big), axis=-1, keepdims=True)
    p_grp = 1.0 / jnp.sum(jnp.exp(gl - gmax), axis=-1, keepdims=True)
    lo = n_grp + grp * epg
    el = jnp.where((lane >= lo) & (lane < lo + epg), logits, neg)
    v1 = jnp.max(el, axis=-1, keepdims=True)
    i1 = jnp.min(jnp.where(el == v1, lane, big), axis=-1, keepdims=True)
    el2 = jnp.where(lane == i1, neg, el)
    v2 = jnp.max(el2, axis=-1, keepdims=True)
    i2 = jnp.min(jnp.where(el2 == v2, lane, big), axis=-1, keepdims=True)
    e2 = jnp.exp(v2 - v1)
    w1 = p_grp / (1.0 + e2)
    w2 = p_grp * e2 / (1.0 + e2)
    ri_ref[...] = jnp.where(lane == 0, i1 - n_grp, jnp.where(lane == 1, i2 - n_grp, 0))
    rw_ref[...] = jnp.where(lane == 0, w1, jnp.where(lane == 1, w2, 0.0))


def _mixout(ym, ys, z_main, us_col, dsk, x2d, wglu, bglu, s5g, wom, wos, g1, b1, wrh, wrl, br,
            alpha, n_grp, epg, tm=512):
    M, D = x2d.shape
    Dm, Ds = ym.shape[1], ys.shape[1]
    tile = lambda w: pl.BlockSpec((tm, w), lambda i: (i, 0))
    full = lambda a: pl.BlockSpec(a.shape, lambda i: (0,) * a.ndim)
    return pl.pallas_call(
        functools.partial(_mixout_kernel, alpha=alpha, n_grp=n_grp, epg=epg),
        grid=(M // tm,),
        in_specs=[tile(Dm), tile(Ds), pl.BlockSpec((tm, Ds), lambda i: (i, us_col // Ds)), full(dsk), tile(D),
                  full(wglu), full(bglu), full(s5g), full(wom), full(wos),
                  full(g1), full(b1), full(wrh), full(wrl), full(br)],
        out_specs=[tile(D), pl.BlockSpec((tm * TOKEN_ROWS, V7X_LANES), lambda i: (i, 0)),
                   tile(V7X_LANES), tile(V7X_LANES)],
        out_shape=[jax.ShapeDtypeStruct((M, D), F32),
                   jax.ShapeDtypeStruct((M * TOKEN_ROWS, V7X_LANES), U32),
                   jax.ShapeDtypeStruct((M, V7X_LANES), I32),
                   jax.ShapeDtypeStruct((M, V7X_LANES), F32)],
        compiler_params=_cparams(("parallel",)),
        name="mixout",
    )(ym, ys, z_main, dsk, x2d, wglu, bglu, s5g, wom, wos, g1, b1, wrh, wrl, br)


def _dispatch_kernel(ri_ref, dest_ref, blk_ref, run_s, pst_s, *, n_exp, rows, nbp):
    ph = pl.program_id(0)
    i = pl.program_id(1)
    tm = ri_ref.shape[0]
    lane = _iota((tm, V7X_LANES), 1)
    oh0 = lane == ri_ref[:, 0:1]
    oh1 = lane == ri_ref[:, 1:2]
    cnt = jnp.where(oh0 | oh1, 1.0, 0.0).astype(F32)
    tile_cnt = jnp.sum(cnt, axis=0, keepdims=True)

    @pl.when((ph == 0) & (i == 0))
    def _():
        run_s[...] = jnp.zeros_like(run_s)

    @pl.when(ph == 0)
    def _():
        run_s[...] = run_s[...] + tile_cnt
        dest_ref[...] = jnp.zeros_like(dest_ref)

    @pl.when((ph == 1) & (i == 0))
    def _():
        counts = run_s[...]
        nblk = jnp.floor((counts + (rows - 1)) / rows)
        tri = jnp.where(_iota((V7X_LANES, V7X_LANES), 0) < _iota((V7X_LANES, V7X_LANES), 1), 1.0, 0.0)
        bstart = jnp.dot(jnp.broadcast_to(nblk, (V7X_SUBLANES, V7X_LANES)), tri.astype(F32),
                         precision=HIGHEST, preferred_element_type=F32)[0:1, :]
        pst_s[...] = bstart * rows
        bend = bstart + nblk
        j = _iota((nbp, V7X_LANES), 0).astype(F32)
        ln = _iota((nbp, V7X_LANES), 1)
        done = jnp.sum(jnp.where((ln < n_exp) & (bend <= j), 1.0, 0.0), axis=1, keepdims=True)
        blk = jnp.minimum(done, n_exp - 1.0)
        used = jnp.sum(jnp.where(ln == n_exp - 1, bend, 0.0), axis=1, keepdims=True)
        mine = ln.astype(F32) == blk
        cnt_j = jnp.sum(jnp.where(mine, counts, 0.0), axis=1, keepdims=True)
        bst_j = jnp.sum(jnp.where(mine, bstart, 0.0), axis=1, keepdims=True)
        valid = jnp.clip(cnt_j - (j[:, 0:1] - bst_j) * rows, 0.0, float(rows))
        blk_ref[...] = jnp.where(ln == 0, blk, jnp.where(ln == 1, used, jnp.where(ln == 2, valid, 0.0))
                                 ).astype(I32)
        run_s[...] = jnp.zeros_like(run_s)

    @pl.when(ph == 1)
    def _():
        lower = jnp.where(_iota((tm, tm), 0) > _iota((tm, tm), 1), 1.0, 0.0).astype(BF16)
        excl = jnp.dot(lower, cnt.astype(BF16), preferred_element_type=F32)
        base = excl + run_s[...] + pst_s[...]
        d0 = jnp.sum(jnp.where(oh0, base, 0.0), axis=1, keepdims=True)
        d1 = jnp.sum(jnp.where(oh1, base, 0.0), axis=1, keepdims=True)
        dest_ref[...] = jnp.where(lane == 0, d0, jnp.where(lane == 1, d1, 0.0)).astype(I32)
        run_s[...] = run_s[...] + tile_cnt


def _dispatch(route_i, n_exp, rows, nbp, tm=512):
    M = route_i.shape[0]
    return pl.pallas_call(
        functools.partial(_dispatch_kernel, n_exp=n_exp, rows=rows, nbp=nbp),
        grid=(2, M // tm),
        in_specs=[pl.BlockSpec((tm, V7X_LANES), lambda p, i: (i, 0))],
        out_specs=[pl.BlockSpec((tm, V7X_LANES), lambda p, i: (i * p, 0)),
                   pl.BlockSpec((nbp, V7X_LANES), lambda p, i: (0, 0))],
        out_shape=[jax.ShapeDtypeStruct((M, V7X_LANES), I32),
                   jax.ShapeDtypeStruct((nbp, V7X_LANES), I32)],
        scratch_shapes=[pltpu.VMEM((1, V7X_LANES), F32), pltpu.VMEM((1, V7X_LANES), F32)],
        compiler_params=_cparams(("arbitrary", "arbitrary")),
        name="dispatch",
    )(route_i)


def _row_copy(src, dst, sem, r_src, r_dst):
    rows = lambda r: pl.ds(pl.multiple_of(r * TOKEN_ROWS, TOKEN_ROWS), TOKEN_ROWS)
    return pltpu.make_async_copy(src.at[rows(r_src), :], dst.at[rows(r_dst), :], sem)


def _scatter_kernel(dest_ref, nvalid_ref, x_ref, xs_out, zbuf, sem, zsem, *, rows):
    tm = x_ref.shape[0] // TOKEN_ROWS
    blk_rows = rows * TOKEN_ROWS
    base = pl.program_id(0) * (tm * TOP_K)

    @pl.when(pl.program_id(0) == 0)
    def _():
        zbuf[...] = jnp.zeros_like(zbuf)
        nblocks = xs_out.shape[0] // blk_rows
        zero_copy = lambda j: pltpu.make_async_copy(zbuf, xs_out.at[pl.ds(j * blk_rows, blk_rows), :], zsem)

        def start(j, _):
            @pl.when(nvalid_ref[j] < rows)
            def _():
                zero_copy(j).start()
            return 0

        def wait(j, _):
            @pl.when(nvalid_ref[j] < rows)
            def _():
                zero_copy(j).wait()
            return 0

        lax.fori_loop(0, nblocks, start, 0)
        lax.fori_loop(0, nblocks, wait, 0)

    def issue(r, _):
        for k in range(TOP_K):
            _row_copy(x_ref, xs_out, sem, r, dest_ref[base + r * TOP_K + k]).start()
        return 0

    lax.fori_loop(0, tm, issue, 0, unroll=8)

    def drain(r, _):
        for k in range(TOP_K):
            _row_copy(x_ref, xs_out, sem, 0, 0).wait()
        return 0

    lax.fori_loop(0, tm, drain, 0, unroll=8)


def _scatter(dest, nvalid, x1p, rows, tm=256):
    M = x1p.shape[0] // TOKEN_ROWS
    return pl.pallas_call(
        functools.partial(_scatter_kernel, rows=rows),
        grid_spec=pltpu.PrefetchScalarGridSpec(
            num_scalar_prefetch=2,
            grid=(M // tm,),
            in_specs=[pl.BlockSpec((tm * TOKEN_ROWS, V7X_LANES), lambda i, d, n: (i, 0))],
            out_specs=pl.BlockSpec(memory_space=pl.ANY),
            scratch_shapes=[pltpu.VMEM((rows * TOKEN_ROWS, V7X_LANES), x1p.dtype),
                            pltpu.SemaphoreType.DMA(()), pltpu.SemaphoreType.DMA(())]),
        out_shape=jax.ShapeDtypeStruct((nvalid.shape[0] * rows * TOKEN_ROWS, V7X_LANES), x1p.dtype),
        compiler_params=_cparams(("arbitrary",)),
        name="scatter",
    )(dest, nvalid, x1p)


def _experts_kernel(blk_ref, used_ref, xs_ref, wg_hbm, wu_hbm, wd_hbm, o_ref,
                    fg, fu, fd, wg_s, wu_s, wd_s, slot_s, wsem, *, layer):
    j = pl.program_id(0)
    used = used_ref[0]
    nblk = pl.num_programs(0)
    active = j < used
    e = blk_ref[j]

    def fetch(expert, slot):
        return [pltpu.make_async_copy(w.at[layer, expert], f.at[slot], wsem.at[slot, t])
                for t, (w, f) in enumerate(((wg_hbm, fg), (wu_hbm, fu), (wd_hbm, fd)))]

    @pl.when(active & (j == 0))
    def _():
        slot_s[0] = 0
        for c in fetch(e, 0):
            c.start()

    @pl.when(active & ((j == 0) | (e != blk_ref[jnp.maximum(j - 1, 0)])))
    def _():
        slot = slot_s[0]
        for c in fetch(e, slot):
            c.wait()
        wg_s[...] = fg[slot].astype(BF16)
        wu_s[...] = fu[slot].astype(BF16)
        wd_s[...] = fd[slot].astype(BF16)
        nxt = lax.while_loop(lambda t: (t < used) & (blk_ref[jnp.minimum(t, nblk - 1)] == e), lambda t: t + 1, j + 1)

        @pl.when(nxt < used)
        def _():
            for c in fetch(blk_ref[jnp.minimum(nxt, nblk - 1)], 1 - slot):
                c.start()

        slot_s[0] = 1 - slot

    @pl.when(active)
    def _():
        lo, hi = _unpack_bf16_pairs(_load_token_rows(xs_ref))
        xb = jnp.concatenate([lo.astype(BF16), hi.astype(BF16)], axis=1)
        g = jnp.dot(xb, wg_s[...], preferred_element_type=F32)
        u = jnp.dot(xb, wu_s[...], preferred_element_type=F32)
        hmid = (g * jax.nn.sigmoid(g) * u).astype(BF16)
        _store_token_rows(o_ref, _pack_bf16_pairs(jnp.dot(hmid, wd_s[...], preferred_element_type=F32)))

    @pl.when(j >= used)
    def _():
        o_ref[...] = jnp.zeros_like(o_ref)


def _experts(blk_e, used, xs, wg, wu, wd, layer, rows):
    P = xs.shape[0] // TOKEN_ROWS
    tile = pl.BlockSpec((rows * TOKEN_ROWS, V7X_LANES), lambda j, b, u: (j, 0))
    D, De = wg.shape[2], wg.shape[3]
    nb = P // rows
    hbm = pl.BlockSpec(memory_space=pl.ANY)
    return pl.pallas_call(
        functools.partial(_experts_kernel, layer=layer),
        grid_spec=pltpu.PrefetchScalarGridSpec(
            num_scalar_prefetch=2,
            grid=(nb,),
            in_specs=[tile, hbm, hbm, hbm],
            out_specs=tile,
            scratch_shapes=[pltpu.VMEM((2, D, De), F32), pltpu.VMEM((2, D, De), F32), pltpu.VMEM((2, De, D), F32),
                            pltpu.VMEM((D, De), BF16), pltpu.VMEM((D, De), BF16), pltpu.VMEM((De, D), BF16),
                            pltpu.SMEM((1,), I32), pltpu.SemaphoreType.DMA((2, 3))]),
        out_shape=jax.ShapeDtypeStruct((P * TOKEN_ROWS, V7X_LANES), U32),
        compiler_params=_cparams(("arbitrary",)),
        name="experts",
    )(blk_e, used, xs, wg, wu, wd)


def _post_kernel(dest_ref, x1_ref, rw_ref, p_ref, os_hbm, g2_ref, b2_ref, wpg_ref, bpg_ref, wpp_ref, pg_ref,
                 out_ref, gbuf, sem, *, alpha):
    i = pl.program_id(0)
    last = pl.num_programs(0) - 1
    tm = x1_ref.shape[0]
    slot = lax.rem(i, 2)

    def gather_row(tile, slot_, r):
        for k in range(TOP_K):
            _row_copy(os_hbm, gbuf.at[slot_, k], sem.at[slot_], dest_ref[tile * (tm * TOP_K) + r * TOP_K + k],
                      r).start()

    def drain(slot_):
        def body(r, _):
            for k in range(TOP_K):
                _row_copy(os_hbm, gbuf.at[slot_, k], sem.at[slot_], 0, 0).wait()
            return 0

        lax.fori_loop(0, tm, body, 0, unroll=8)

    @pl.when(i == 0)
    def _():
        def first(r, _):
            gather_row(0, 0, r)
            return 0

        lax.fori_loop(0, tm, first, 0, unroll=8)

    drain(slot)
    nxt = jnp.minimum(i + 1, last)
    for r in range(tm):
        gather_row(nxt, 1 - slot, r)

    rw = rw_ref[...]
    lo0, hi0 = _unpack_bf16_pairs(_load_token_rows(gbuf.at[slot, 0]))
    lo1, hi1 = _unpack_bf16_pairs(_load_token_rows(gbuf.at[slot, 1]))
    w0, w1 = rw[:, 0:1], rw[:, 1:2]
    ffn = jnp.concatenate([w0 * lo0 + w1 * lo1, w0 * hi0 + w1 * hi1], axis=1)
    x2 = _layer_norm(alpha * x1_ref[...] + ffn, g2_ref[...], b2_ref[...])
    gate = jax.nn.sigmoid(jnp.dot(x2.astype(BF16), wpg_ref[...], preferred_element_type=F32) + bpg_ref[...])
    pp = jnp.dot(p_ref[...].astype(BF16), wpp_ref[...], preferred_element_type=F32)
    ple = pp * lax.rsqrt(jnp.mean(pp * pp, axis=-1, keepdims=True) + EPS) * pg_ref[...]
    out_ref[...] = x2 + gate * ple

    @pl.when(i == last)
    def _():
        drain(1 - slot)


def _post(dest, x1, route_w, p3d, layer, os, g2, b2, wpg, bpg, wpp, pg, alpha, tm=256):
    M, D = x1.shape
    Dp = p3d.shape[2]
    tile = lambda w: pl.BlockSpec((tm, w), lambda i, d: (i, 0))
    full = lambda a: pl.BlockSpec(a.shape, lambda i, d: (0,) * a.ndim)
    return pl.pallas_call(
        functools.partial(_post_kernel, alpha=alpha),
        grid_spec=pltpu.PrefetchScalarGridSpec(
            num_scalar_prefetch=1,
            grid=(M // tm,),
            in_specs=[tile(D), tile(V7X_LANES), pl.BlockSpec((None, tm, Dp), lambda i, d: (layer, i, 0)),
                      pl.BlockSpec(memory_space=pl.ANY),
                      full(g2), full(b2), full(wpg), full(bpg), full(wpp), full(pg)],
            out_specs=tile(D),
            scratch_shapes=[pltpu.VMEM((2, TOP_K, tm * TOKEN_ROWS, V7X_LANES), U32), pltpu.SemaphoreType.DMA((2,))]),
        out_shape=jax.ShapeDtypeStruct((M, D), F32),
        compiler_params=_cparams(("arbitrary",)),
        name="post",
    )(dest, x1, route_w, p3d, os, g2, b2, wpg, bpg, wpp, pg)


def kernel(x, p, w_in, conv_w, conv_b, w_q, w_k, b_i, b_f, mh_g, lam_re, lam_im, log_dt, b_re, b_im, c_re, c_im, d_skip, w_glu, b_glu, s5_g, w_out, ln1_g, ln1_b, w_grp, b_grp, w_rt, b_rt, w_eg, w_eu, w_ed, ln2_g, ln2_b, w_pg, b_pg, w_pp, ple_g):
    B, S, D = x.shape
    depth = w_in.shape[0]
    H, Dh = w_q.shape[1], w_q.shape[2]
    Dm = H * Dh
    G, Cn = d_skip.shape[1], d_skip.shape[2]
    Ds = G * Cn
    n_grp = w_grp.shape[-1]
    n_exp = w_eg.shape[1]
    epg = n_exp // n_grp
    M = B * S
    A = M * TOP_K
    alpha = (2 * depth) ** 0.25
    NC = S // MLSTM_CHUNK
    nblocks = A // MOE_ROWS + n_exp
    nbp = -(-nblocks // V7X_SUBLANES) * V7X_SUBLANES
    row2 = lambda a: a.reshape(1, -1).astype(F32)

    s5_all = _s5_operators(lam_re, lam_im, log_dt, b_re, b_im, c_re, c_im, S5_CHUNK)

    xc = x.reshape(M, D).astype(F32)
    for l in range(depth):
        wi = w_in[l]
        w_main = jnp.concatenate([wi[:, :3 * Dm], wi[:, 3 * Dm + 2 * H:]], axis=1).astype(BF16)
        w_gate = jnp.pad(wi[:, 3 * Dm:3 * Dm + 2 * H], ((0, 0), (0, V7X_LANES - 2 * H))).astype(BF16)
        w_router = jnp.pad(jnp.concatenate([w_grp[l], w_rt[l]], axis=1).astype(F32),
                           ((0, 0), (0, V7X_LANES - n_grp - n_exp)))
        w_router_hi = w_router.astype(BF16)
        w_router_lo = (w_router - w_router_hi.astype(F32)).astype(BF16)
        b_router = jnp.pad(jnp.concatenate([b_grp[l], b_rt[l]]).astype(F32),
                           (0, V7X_LANES - n_grp - n_exp)).reshape(1, -1)

        z_main, z_gate = _inproj(xc, w_main, w_gate)

        gates = z_gate[:, :2 * H].reshape(B, NC, MLSTM_CHUNK, 2, H).transpose(0, 4, 3, 1, 2)
        y_m = _mlstm(z_main.reshape(B, S, -1), gates, jnp.stack([b_i[l], b_f[l]]).astype(F32),
                     conv_w[l].astype(F32), row2(conv_b[l]), w_q[l].astype(BF16), w_k[l].astype(BF16),
                     row2(mh_g[l]), H, Dh).reshape(M, Dm)

        y_s = _s5(z_main.reshape(B, S, -1), 3 * Dm, *s5_all, l, Cn, S5_CHUNK).reshape(M, Ds)

        wo = w_out[l].astype(BF16)
        x1, x1p, route_i, route_w = _mixout(y_m, y_s, z_main, 3 * Dm, row2(d_skip[l]), xc,
                                       w_glu[l].astype(BF16), row2(b_glu[l]), row2(s5_g[l]),
                                       wo[:Dm], wo[Dm:], row2(ln1_g[l]), row2(ln1_b[l]),
                                       w_router_hi, w_router_lo, b_router, alpha, n_grp, epg)

        dest_l, blk_l = _dispatch(route_i, n_exp, MOE_ROWS, nbp)
        dest = dest_l[:, :TOP_K].reshape(A)
        xs = _scatter(dest, blk_l[:nblocks, 2], x1p, MOE_ROWS)
        os = _experts(blk_l[:nblocks, 0], blk_l[:1, 1], xs, w_eg, w_eu, w_ed, l, MOE_ROWS)
        xc = _post(dest, x1, route_w, p.reshape(depth, M, -1), l, os, row2(ln2_g[l]), row2(ln2_b[l]),
                   w_pg[l].astype(BF16), row2(b_pg[l]), w_pp[l].astype(BF16), row2(ple_g[l]), alpha)
    return xc.reshape(B, S, D).astype(x.dtype)
```
